```python
import math
import jax
import jax.numpy as jnp
from jax import lax
import numpy as np

D_MODEL = 2048
BATCH = 2
SEQ = 16384
DEPTH = 2

FFN_DIM = 5632
GDN_HEADS = 8
GDN_DK = 128
GDN_DV = 128
GDN_CONV = 4
GDN_CHUNK = 64
SB_HEADS = 8
SB_DH = 128
SB_BLOCK = 128
HGRN_EXPAND = 128
HGRN_HEADS = D_MODEL // HGRN_EXPAND
HGRN_DK = HGRN_EXPAND
HGRN_DV = D_MODEL // HGRN_HEADS
HGRN_CHUNK = 64
EPS = 1e-6

GDN_QK = GDN_HEADS * GDN_DK
GDN_VW = GDN_HEADS * GDN_DV
SB_W = SB_HEADS * SB_DH
GDN_CONV_CH = 2 * GDN_QK + GDN_VW
EVEN_IN = GDN_CONV_CH + GDN_VW + 2 * GDN_HEADS + 3 * SB_W
EVEN_MIX = GDN_VW + SB_W
HGRN_KW = HGRN_HEADS * HGRN_DK
HGRN_VW = HGRN_HEADS * HGRN_DV
ODD_IN = 2 * HGRN_KW + 2 * HGRN_VW
N_EVEN = (DEPTH + 1) // 2
N_ODD = DEPTH // 2

kernel_name = "hybrid_gdn_stickbreak_hgrn2_macaron"


def rms_norm(x, w):
    xf = x.astype(jnp.float32)
    y = xf * lax.rsqrt(jnp.mean(xf * xf, axis=-1, keepdims=True) + EPS)
    return (y * w.astype(jnp.float32)).astype(x.dtype)


def l2_norm(x):
    xf = x.astype(jnp.float32)
    return xf * lax.rsqrt(jnp.sum(xf * xf, axis=-1, keepdims=True) + EPS)


def swiglu(h, w_gate, w_up, w_down):
    return (jax.nn.silu(h @ w_gate) * (h @ w_up)) @ w_down


def causal_short_conv(x, w):
    ksz = w.shape[0]
    seq = x.shape[1]
    xp = jnp.pad(x, ((0, 0), (ksz - 1, 0), (0, 0)))
    out = xp[:, 0:seq] * w[0]
    for j in range(1, ksz):
        out = out + xp[:, j:j + seq] * w[j]
    return out


def to_heads(t, n_heads):
    bsz, seq, _ = t.shape
    return t.reshape(bsz, seq, n_heads, -1).transpose(0, 2, 1, 3)


def gated_delta_rule(q, k, v, g, beta):
    bsz, nh, seq, dk = q.shape
    dv = v.shape[-1]
    c = GDN_CHUNK
    n = seq // c
    f32 = jnp.float32
    q = (q.astype(f32) * dk ** -0.5).reshape(bsz, nh, n, c, dk)
    k = k.astype(f32).reshape(bsz, nh, n, c, dk)
    v = v.astype(f32).reshape(bsz, nh, n, c, dv)
    g = g.astype(f32).reshape(bsz, nh, n, c)
    beta = beta.astype(f32).reshape(bsz, nh, n, c)
    gc = jnp.cumsum(g, axis=-1)
    idx = jnp.arange(c)
    causal = idx[:, None] >= idx[None, :]
    strict = idx[:, None] > idx[None, :]
    decay = jnp.exp(jnp.where(causal, gc[..., :, None] - gc[..., None, :], -jnp.inf))
    kb = k * beta[..., None]
    vb = v * beta[..., None]
    lower = jnp.where(strict, jnp.einsum('bhnid,bhnjd->bhnij', kb, k) * decay, 0.0)
    eye = jnp.eye(c, dtype=f32)
    tinv = lax.linalg.triangular_solve(eye + lower, jnp.broadcast_to(eye, lower.shape),
                                       left_side=True, lower=True, unit_diagonal=True)
    u = jnp.einsum('bhnij,bhnjd->bhnid', tinv, vb)
    w = jnp.einsum('bhnij,bhnjd->bhnid', tinv, kb * jnp.exp(gc)[..., None])
    a_qk = jnp.einsum('bhnid,bhnjd->bhnij', q, k) * decay
    q_dec = q * jnp.exp(gc)[..., None]
    k_dec = k * jnp.exp(gc[..., -1:] - gc)[..., None]
    g_last = jnp.exp(gc[..., -1])

    def step(state, xs):
        w_c, u_c, qd_c, kd_c, a_c, gl_c = xs
        v_new = u_c - jnp.einsum('bhid,bhde->bhie', w_c, state)
        o_c = jnp.einsum('bhid,bhde->bhie', qd_c, state) + jnp.einsum('bhij,bhje->bhie', a_c, v_new)
        state = state * gl_c[..., None, None] + jnp.einsum('bhjd,bhje->bhde', kd_c, v_new)
        return state, o_c

    def chunk_first(t):
        return jnp.moveaxis(t, 2, 0)

    s0 = jnp.zeros((bsz, nh, dk, dv), f32)
    _, o = lax.scan(step, s0, (chunk_first(w), chunk_first(u), chunk_first(q_dec),
                               chunk_first(k_dec), chunk_first(a_qk), chunk_first(g_last)))
    return jnp.moveaxis(o, 0, 2).reshape(bsz, nh, seq, dv)


def stick_breaking_attention(q, k, v):
    bsz, nh, seq, dh = q.shape
    f32 = jnp.float32
    q = q.astype(f32) * dh ** -0.5
    k = k.astype(f32)
    v = v.astype(f32)
    nblk = seq // SB_BLOCK
    ar = jnp.arange(SB_BLOCK)

    def query_block(i):
        q_blk = lax.dynamic_slice_in_dim(q, i * SB_BLOCK, SB_BLOCK, axis=2)
        q_pos = i * SB_BLOCK + ar

        def key_block(n, carry):
            acc, log_surv = carry
            j = i - n
            k_blk = lax.dynamic_slice_in_dim(k, j * SB_BLOCK, SB_BLOCK, axis=2)
            v_blk = lax.dynamic_slice_in_dim(v, j * SB_BLOCK, SB_BLOCK, axis=2)
            mask = (j * SB_BLOCK + ar)[None, :] < q_pos[:, None]
            z = jnp.einsum('bhqd,bhkd->bhqk', q_blk, k_blk)
            log_beta = jax.nn.log_sigmoid(z)
            log_fail = jnp.where(mask, log_beta - z, 0.0)
            log_later = lax.cumsum(log_fail, axis=3, reverse=True) - log_fail + log_surv[..., None]
            wts = jnp.where(mask, jnp.exp(log_beta + log_later), 0.0)
            acc = acc + jnp.einsum('bhqk,bhkd->bhqd', wts, v_blk)
            return acc, log_surv + jnp.sum(log_fail, axis=3)

        init = (jnp.zeros((bsz, nh, SB_BLOCK, dh), f32), jnp.zeros((bsz, nh, SB_BLOCK), f32))
        acc, _ = lax.fori_loop(0, i + 1, key_block, init)
        return acc

    o = lax.map(query_block, jnp.arange(nblk))
    return jnp.moveaxis(o, 0, 2).reshape(bsz, nh, seq, dh)


def hgrn2_recurrence(q, k, v, log_f):
    bsz, nh, seq, dk = q.shape
    dv = v.shape[-1]
    c = HGRN_CHUNK
    n = seq // c
    f32 = jnp.float32

    def chunked(t):
        return jnp.moveaxis(t.astype(f32).reshape(bsz, nh, n, c, t.shape[-1]), 2, 0)

    idx = jnp.arange(c)
    causal = (idx[:, None] >= idx[None, :])[:, :, None]

    def step(state, xs):
        q_c, k_c, v_c, lf_c = xs
        gc = jnp.cumsum(lf_c, axis=2)
        decay = jnp.exp(jnp.where(causal, gc[:, :, :, None, :] - gc[:, :, None, :, :], -jnp.inf))
        a_c = jnp.einsum('bhid,bhjd,bhijd->bhij', q_c, k_c, decay)
        o_c = jnp.einsum('bhid,bhde->bhie', q_c * jnp.exp(gc), state) + jnp.einsum('bhij,bhje->bhie', a_c, v_c)
        g_last = gc[:, :, -1:, :]
        state = state * jnp.exp(g_last[:, :, 0, :])[..., None] + jnp.einsum('bhjd,bhje->bhde', k_c * jnp.exp(g_last - gc), v_c)
        return state, o_c

    s0 = jnp.zeros((bsz, nh, dk, dv), f32)
    _, o = lax.scan(step, s0, (chunked(q), chunked(k), chunked(v), chunked(log_f)))
    return jnp.moveaxis(o, 0, 2).reshape(bsz, nh, seq, dv)


def even_mixer(u, w_in, conv_w, a_log, dt_bias, out_norm, w_out):
    bsz, seq, _ = u.shape
    f32 = jnp.float32
    proj = u @ w_in
    o1 = GDN_CONV_CH
    o2 = o1 + GDN_VW
    o3 = o2 + GDN_HEADS
    o4 = o3 + GDN_HEADS
    o5 = o4 + SB_W
    o6 = o5 + SB_W
    qkv_a, z_a, b_a, a_a, q_b, k_b, v_b = jnp.split(proj, [o1, o2, o3, o4, o5, o6], axis=-1)
    qkv_a = jax.nn.silu(causal_short_conv(qkv_a, conv_w))
    q_a, k_a, v_a = jnp.split(qkv_a, [GDN_QK, 2 * GDN_QK], axis=-1)
    q_a = l2_norm(to_heads(q_a, GDN_HEADS))
    k_a = l2_norm(to_heads(k_a, GDN_HEADS))
    v_a = to_heads(v_a, GDN_HEADS)
    beta = jax.nn.sigmoid(b_a.astype(f32)).transpose(0, 2, 1)
    g = (-jnp.exp(a_log.astype(f32)) * jax.nn.softplus(a_a.astype(f32) + dt_bias.astype(f32))).transpose(0, 2, 1)
    o_a = gated_delta_rule(q_a, k_a, v_a, g, beta).transpose(0, 2, 1, 3)
    z_h = z_a.reshape(bsz, seq, GDN_HEADS, GDN_DV).astype(f32)
    o_a = (rms_norm(o_a, out_norm) * jax.nn.silu(z_h)).reshape(bsz, seq, GDN_VW).astype(u.dtype)
    o_b = stick_breaking_attention(to_heads(q_b, SB_HEADS), to_heads(k_b, SB_HEADS), to_heads(v_b, SB_HEADS))
    o_b = o_b.transpose(0, 2, 1, 3).reshape(bsz, seq, SB_W).astype(u.dtype)
    return jnp.concatenate([o_a, o_b], axis=-1) @ w_out


def odd_mixer(u, w_in, lower_bound, out_norm, w_out):
    bsz, seq, _ = u.shape
    f32 = jnp.float32
    proj = u @ w_in
    q, f, i_in, gate = jnp.split(proj, [HGRN_KW, 2 * HGRN_KW, 2 * HGRN_KW + HGRN_VW], axis=-1)
    ff = f.astype(f32)
    lb = lower_bound.astype(f32)
    log_f = jnp.logaddexp(jnp.log(lb), jnp.log1p(-lb) + jax.nn.log_sigmoid(ff))
    k = (1.0 - lb) * jax.nn.sigmoid(-ff)
    q = jax.nn.silu(q.astype(f32))
    o = hgrn2_recurrence(to_heads(q, HGRN_HEADS), to_heads(k, HGRN_HEADS),
                         to_heads(i_in, HGRN_HEADS), to_heads(log_f, HGRN_HEADS))
    o = o.transpose(0, 2, 1, 3)
    g_h = gate.reshape(bsz, seq, HGRN_HEADS, HGRN_DV).astype(f32)
    o = (rms_norm(o, out_norm) * jax.nn.silu(g_h)).reshape(bsz, seq, HGRN_VW).astype(u.dtype)
    return o @ w_out


def setup_inputs(seed: int = 0) -> dict:
    key = jax.random.key(seed)
    ks = jax.random.split(key, 24)
    f32 = jnp.float32

    def nrm(k, shape, scale):
        return jax.random.normal(k, shape, f32) * scale

    def gain(k, shape):
        return 1.0 + 0.02 * jax.random.normal(k, shape, f32)

    dt = jnp.exp(jax.random.uniform(ks[13], (N_EVEN, GDN_HEADS), f32, math.log(1e-3), math.log(1e-1)))
    return {
        "x": nrm(ks[0], (BATCH, SEQ, D_MODEL), 1.0),
        "ffn1_norm": gain(ks[1], (DEPTH, D_MODEL)),
        "ffn1_w_gate": nrm(ks[2], (DEPTH, D_MODEL, FFN_DIM), D_MODEL ** -0.5),
        "ffn1_w_up": nrm(ks[3], (DEPTH, D_MODEL, FFN_DIM), D_MODEL ** -0.5),
        "ffn1_w_down": nrm(ks[4], (DEPTH, FFN_DIM, D_MODEL), FFN_DIM ** -0.5),
        "mix_norm": gain(ks[5], (DEPTH, D_MODEL)),
        "ffn2_norm": gain(ks[6], (DEPTH, D_MODEL)),
        "ffn2_w_gate": nrm(ks[7], (DEPTH, D_MODEL, FFN_DIM), D_MODEL ** -0.5),
        "ffn2_w_up": nrm(ks[8], (DEPTH, D_MODEL, FFN_DIM), D_MODEL ** -0.5),
        "ffn2_w_down": nrm(ks[9], (DEPTH, FFN_DIM, D_MODEL), FFN_DIM ** -0.5),
        "even_w_in": nrm(ks[10], (N_EVEN, D_MODEL, EVEN_IN), D_MODEL ** -0.5),
        "gdn_conv_w": nrm(ks[11], (N_EVEN, GDN_CONV, GDN_CONV_CH), GDN_CONV ** -0.5),
        "gdn_a_log": jnp.log(jax.random.uniform(ks[12], (N_EVEN, GDN_HEADS), f32, 1.0, 16.0)),
        "gdn_dt_bias": dt + jnp.log(-jnp.expm1(-dt)),
        "gdn_out_norm": gain(ks[14], (N_EVEN, GDN_DV)),
        "even_w_out": nrm(ks[15], (N_EVEN, EVEN_MIX, D_MODEL), EVEN_MIX ** -0.5),
        "odd_w_in": nrm(ks[16], (N_ODD, D_MODEL, ODD_IN), D_MODEL ** -0.5),
        "hgrn_lower_bounds": nrm(ks[17], (DEPTH, HGRN_KW), 0.1),
        "hgrn_out_norm": gain(ks[18], (N_ODD, HGRN_DV)),
        "odd_w_out": nrm(ks[19], (N_ODD, HGRN_VW, D_MODEL), HGRN_VW ** -0.5),
        "final_norm": gain(ks[20], (D_MODEL,)),
    }


def reference(x, ffn1_norm, ffn1_w_gate, ffn1_w_up, ffn1_w_down, mix_norm,
              ffn2_norm, ffn2_w_gate, ffn2_w_up, ffn2_w_down,
              even_w_in, gdn_conv_w, gdn_a_log, gdn_dt_bias, gdn_out_norm, even_w_out,
              odd_w_in, hgrn_lower_bounds, hgrn_out_norm, odd_w_out, final_norm):
    lb_soft = jax.nn.softmax(hgrn_lower_bounds.astype(jnp.float32), axis=0)
    lb_all = jnp.cumsum(lb_soft, axis=0) - lb_soft[0]
    h = x
    for layer in range(DEPTH):
        h = h + 0.5 * swiglu(rms_norm(h, ffn1_norm[layer]), ffn1_w_gate[layer], ffn1_w_up[layer], ffn1_w_down[layer])
        u = rms_norm(h, mix_norm[layer])
        if layer % 2 == 0:
            e = layer // 2
            mix = even_mixer(u, even_w_in[e], gdn_conv_w[e], gdn_a_log[e], gdn_dt_bias[e],
                             gdn_out_norm[e], even_w_out[e])
        else:
            o = layer // 2
            mix = odd_mixer(u, odd_w_in[o], lb_all[layer], hgrn_out_norm[o], odd_w_out[o])
        h = h + mix.astype(h.dtype)
        h = h + 0.5 * swiglu(rms_norm(h, ffn2_norm[layer]), ffn2_w_gate[layer], ffn2_w_up[layer], ffn2_w_down[layer])
    return rms_norm(h, final_norm)
```

```python
import functools
import math

import jax
import jax.numpy as jnp
from jax import lax
from jax.experimental import pallas as pl
from jax.experimental.pallas import tpu as pltpu

F32 = jnp.float32
BF16 = jnp.bfloat16
EPS = 1e-6

HEAD = 128
GDN_HEADS = 8
SB_HEADS = 8
CHUNK = 64
SB_BLOCK = 128
SUBLANES = 8

FFN_ROWS = 512
FFN_COLS = 512
PROJ_ROWS = 512
OUT_ROWS = 256
HGRN_GROUP = 4
VMEM_LIMIT = 48 * 1024 * 1024


def _mm(a, b):
    return jnp.dot(a.astype(BF16), b.astype(BF16), preferred_element_type=F32)


def _mm_nt(a, b):
    return lax.dot_general(a.astype(BF16), b.astype(BF16), (((1,), (1,)), ((), ())),
                           preferred_element_type=F32)


def _mm_tn(a, b):
    return lax.dot_general(a.astype(BF16), b.astype(BF16), (((0,), (0,)), ((), ())),
                           preferred_element_type=F32)


def _split(x):
    hi = x.astype(BF16)
    lo = (x - hi.astype(F32)).astype(BF16)
    return hi, lo


def _mm_hi(a, b):
    ah, al = _split(a)
    bh, bl = _split(b)
    dot = functools.partial(jnp.dot, preferred_element_type=F32)
    return dot(ah, bh) + (dot(ah, bl) + dot(al, bh))


def _rms(x, w):
    return x * lax.rsqrt(jnp.mean(x * x, axis=-1, keepdims=True) + EPS) * w


def _silu(x):
    return x * jax.nn.sigmoid(x)


def _softplus(x):
    return jnp.maximum(x, 0.0) + jnp.log1p(jnp.exp(-jnp.abs(x)))


def _log_sigmoid(x):
    return jnp.minimum(x, 0.0) - jnp.log1p(jnp.exp(-jnp.abs(x)))


def _cumsum_rows(x):
    row = lax.broadcasted_iota(jnp.int32, x.shape, 0)
    s = 1
    while s < x.shape[0]:
        x = x + jnp.where(row >= s, pltpu.roll(x, s, 0), 0.0)
        s *= 2
    return x


def _shift_rows(x, prev, s):
    y = pltpu.roll(x, s, 0)
    p = pltpu.roll(prev, s, 0)
    row = lax.broadcasted_iota(jnp.int32, prev.shape, 0)
    top = jnp.where(row < s, p, y[:SUBLANES])
    return jnp.concatenate([top, y[SUBLANES:]], axis=0)


def _ffn_kernel(x_ref, nw_ref, wg_ref, wu_ref, wd_ref, fw_ref, o_ref, xn_ref, *, final_norm):
    j = pl.program_id(1)

    @pl.when(j == 0)
    def _():
        xn_ref[...] = _rms(x_ref[...], nw_ref[...]).astype(BF16)
        o_ref[...] = jnp.zeros_like(o_ref)

    xn = xn_ref[...]
    g = jnp.dot(xn, wg_ref[...], preferred_element_type=F32)
    u = jnp.dot(xn, wu_ref[...], preferred_element_type=F32)
    o_ref[...] += _mm(_silu(g) * u, wd_ref[...])

    @pl.when(j == pl.num_programs(1) - 1)
    def _():
        y = x_ref[...] + 0.5 * o_ref[...]
        if final_norm:
            y = _rms(y, fw_ref[...])
        o_ref[...] = y


def _ffn(h, norm_w, w_gate, w_up, w_down, final_w=None):
    n, d = h.shape
    f = w_gate.shape[1]
    tm, tf = min(FFN_ROWS, n), FFN_COLS
    fw = norm_w if final_w is None else final_w
    return pl.pallas_call(
        functools.partial(_ffn_kernel, final_norm=final_w is not None),
        grid=(n // tm, f // tf),
        in_specs=[
            pl.BlockSpec((tm, d), lambda i, j: (i, 0)),
            pl.BlockSpec((1, d), lambda i, j: (0, 0)),
            pl.BlockSpec((d, tf), lambda i, j: (0, j)),
            pl.BlockSpec((d, tf), lambda i, j: (0, j)),
            pl.BlockSpec((tf, d), lambda i, j: (j, 0)),
            pl.BlockSpec((1, d), lambda i, j: (0, 0)),
        ],
        out_specs=pl.BlockSpec((tm, d), lambda i, j: (i, 0)),
        out_shape=jax.ShapeDtypeStruct((n, d), F32),
        scratch_shapes=[pltpu.VMEM((tm, d), BF16)],
        compiler_params=pltpu.CompilerParams(
            dimension_semantics=("parallel", "arbitrary"), vmem_limit_bytes=VMEM_LIMIT),
        name="ffn",
    )(h, norm_w.reshape(1, d), w_gate.astype(BF16), w_up.astype(BF16), w_down.astype(BF16),
      fw.reshape(1, d))


def _norm_proj_kernel(x_ref, nw_ref, w_ref, cs_ref, o_ref, xn_ref, *, scaled):
    @pl.when(pl.program_id(1) == 0)
    def _():
        xn_ref[...] = _rms(x_ref[...], nw_ref[...]).astype(BF16)

    y = jnp.dot(xn_ref[...], w_ref[...], preferred_element_type=F32)
    if scaled:
        y = y * cs_ref[...]
    o_ref[...] = y.astype(o_ref.dtype)


def _norm_proj(h, norm_w, w, out_dtype, tn, col_scale=None):
    n, d = h.shape
    c = w.shape[1]
    tm = min(PROJ_ROWS, n)
    cs = jnp.ones((1, c), F32) if col_scale is None else col_scale.reshape(1, c)
    return pl.pallas_call(
        functools.partial(_norm_proj_kernel, scaled=col_scale is not None),
        grid=(n // tm, c // tn),
        in_specs=[
            pl.BlockSpec((tm, d), lambda i, j: (i, 0)),
            pl.BlockSpec((1, d), lambda i, j: (0, 0)),
            pl.BlockSpec((d, tn), lambda i, j: (0, j)),
            pl.BlockSpec((1, tn), lambda i, j: (0, j)),
        ],
        out_specs=pl.BlockSpec((tm, tn), lambda i, j: (i, j)),
        out_shape=jax.ShapeDtypeStruct((n, c), out_dtype),
        scratch_shapes=[pltpu.VMEM((tm, d), BF16)],
        compiler_params=pltpu.CompilerParams(
            dimension_semantics=("parallel", "arbitrary"), vmem_limit_bytes=VMEM_LIMIT),
        name="norm_proj",
    )(h, norm_w.reshape(1, d), w.astype(BF16), cs)


def _out_proj_kernel(*refs):
    h_ref, o_ref = refs[0], refs[-1]
    k = (len(refs) - 2) // 2
    acc = h_ref[...]
    for x_ref, w_ref in zip(refs[1:1 + k], refs[1 + k:1 + 2 * k]):
        acc = acc + _mm(x_ref[...], w_ref[...])
    o_ref[...] = acc


def _out_proj(h, xs, ws):
    n, d = h.shape
    tm = min(OUT_ROWS, n)
    return pl.pallas_call(
        _out_proj_kernel,
        grid=(n // tm,),
        in_specs=([pl.BlockSpec((tm, d), lambda i: (i, 0))]
                  + [pl.BlockSpec((tm, x.shape[1]), lambda i: (i, 0)) for x in xs]
                  + [pl.BlockSpec(w.shape, lambda i: (0, 0)) for w in ws]),
        out_specs=pl.BlockSpec((tm, d), lambda i: (i, 0)),
        out_shape=jax.ShapeDtypeStruct((n, d), F32),
        compiler_params=pltpu.CompilerParams(
            dimension_semantics=("parallel",), vmem_limit_bytes=VMEM_LIMIT),
        name="out_proj",
    )(h, *xs, *[w.astype(BF16) for w in ws])


def _unit_lower_inverse(low):
    c = low.shape[0]
    row = lax.broadcasted_iota(jnp.int32, (c, c), 0)
    col = lax.broadcasted_iota(jnp.int32, (c, c), 1)
    p = -low
    t = jnp.where(row == col, 1.0, 0.0) + p
    s = 2
    while s < c:
        p = _mm_hi(p, p)
        t = t + _mm_hi(t, p)
        s *= 2
    return t


def _gdn_kernel(q_ref, k_ref, v_ref, z_ref, ba_ref, cw_ref, alog_ref, dtb_ref, onw_ref,
                o_ref, s_ref, prev_ref):
    c = pl.program_id(1)
    nh, w = GDN_HEADS, GDN_HEADS * HEAD

    @pl.when(c == 0)
    def _():
        s_ref[...] = jnp.zeros_like(s_ref)
        prev_ref[...] = jnp.zeros_like(prev_ref)

    x = jnp.concatenate([q_ref[...], k_ref[...], v_ref[...]], axis=1)
    prev = prev_ref[...]
    cw = cw_ref[...]
    ntap = cw.shape[0]
    conv = None
    for j in range(ntap):
        s = ntap - 1 - j
        term = (x if s == 0 else _shift_rows(x, prev, s)) * cw[j:j + 1, :]
        conv = term if conv is None else conv + term
    prev_ref[...] = x[CHUNK - SUBLANES:, :]
    qkv = _silu(conv)

    ba = ba_ref[...]
    beta_t = jax.nn.sigmoid(ba)
    g_t = -jnp.exp(alog_ref[...]) * _softplus(ba + dtb_ref[...])
    gc_t = _cumsum_rows(g_t)
    gc_rows = gc_t.T

    row = lax.broadcasted_iota(jnp.int32, (CHUNK, CHUNK), 0)
    col = lax.broadcasted_iota(jnp.int32, (CHUNK, CHUNK), 1)
    causal = row >= col
    strict = row > col
    z = z_ref[...]
    onw = onw_ref[...]

    for h in range(nh):
        lanes = slice(h * HEAD, (h + 1) * HEAD)
        qh = qkv[:, h * HEAD:(h + 1) * HEAD]
        kh = qkv[:, w + h * HEAD:w + (h + 1) * HEAD]
        vh = qkv[:, 2 * w + h * HEAD:2 * w + (h + 1) * HEAD]
        qh = qh * lax.rsqrt(jnp.sum(qh * qh, axis=-1, keepdims=True) + EPS) * (HEAD ** -0.5)
        kh = kh * lax.rsqrt(jnp.sum(kh * kh, axis=-1, keepdims=True) + EPS)

        beta = beta_t[:, h:h + 1]
        gcol = gc_t[:, nh + h:nh + h + 1]
        grow = gc_rows[nh + h:nh + h + 1, :]
        glast = gcol[CHUNK - 1:CHUNK, :]
        decay = jnp.where(causal, jnp.exp(jnp.where(causal, gcol - grow, 0.0)), 0.0)

        kb = kh * beta
        vb = vh * beta
        low = jnp.where(strict, _mm_nt(kb, kh) * decay, 0.0)
        tinv = _unit_lower_inverse(low)
        u = _mm(tinv, vb)
        wmat = _mm(tinv, kb * jnp.exp(gcol))
        a_qk = _mm_nt(qh, kh) * decay
        q_dec = qh * jnp.exp(gcol)
        k_dec = kh * jnp.exp(glast - gcol)

        state = s_ref[h]
        v_new = u - _mm(wmat, state)
        o = _mm(q_dec, state) + _mm(a_qk, v_new)
        s_ref[h] = state * jnp.exp(glast) + _mm_tn(k_dec, v_new)

        o_ref[:, lanes] = _rms(o, onw) * _silu(z[:, lanes])


def _gdn(proj, conv_w, a_log, dt_bias, out_norm, batch, seq):
    n = proj.shape[0]
    nh, w = GDN_HEADS, GDN_HEADS * HEAD
    nc = seq // CHUNK
    gate_block = 4 * w // HEAD
    pad = lambda p: jnp.zeros((1, HEAD), F32).at[0, nh:2 * nh].set(p.astype(F32))
    rows = lambda b, c: b * nc + c
    return pl.pallas_call(
        _gdn_kernel,
        grid=(batch, nc),
        in_specs=[
            pl.BlockSpec((CHUNK, w), lambda b, c: (rows(b, c), 0)),
            pl.BlockSpec((CHUNK, w), lambda b, c: (rows(b, c), 1)),
            pl.BlockSpec((CHUNK, w), lambda b, c: (rows(b, c), 2)),
            pl.BlockSpec((CHUNK, w), lambda b, c: (rows(b, c), 3)),
            pl.BlockSpec((CHUNK, HEAD), lambda b, c: (rows(b, c), gate_block)),
            pl.BlockSpec(conv_w.shape, lambda b, c: (0, 0)),
            pl.BlockSpec((1, HEAD), lambda b, c: (0, 0)),
            pl.BlockSpec((1, HEAD), lambda b, c: (0, 0)),
            pl.BlockSpec((1, HEAD), lambda b, c: (0, 0)),
        ],
        out_specs=pl.BlockSpec((CHUNK, w), lambda b, c: (rows(b, c), 0)),
        out_shape=jax.ShapeDtypeStruct((n, w), F32),
        scratch_shapes=[pltpu.VMEM((nh, HEAD, HEAD), F32), pltpu.VMEM((SUBLANES, 3 * w), F32)],
        compiler_params=pltpu.CompilerParams(
            dimension_semantics=("parallel", "arbitrary"), vmem_limit_bytes=VMEM_LIMIT),
        name="gdn",
    )(proj, proj, proj, proj, proj, conv_w, pad(a_log), pad(dt_bias), out_norm.reshape(1, HEAD))


def _sb_kernel(q_ref, k_ref, v_ref, uo_ref, o_ref):
    i = pl.program_id(2)
    q = q_ref[...]
    uo = uo_ref[...]
    row = lax.broadcasted_iota(jnp.int32, (SB_BLOCK, SB_BLOCK), 0)
    col = lax.broadcasted_iota(jnp.int32, (SB_BLOCK, SB_BLOCK), 1)
    earlier = col < row

    def key_block(j, acc, surv, mask):
        start = pl.multiple_of(j * SB_BLOCK, SB_BLOCK)
        k = k_ref[pl.ds(start, SB_BLOCK), :]
        v = v_ref[pl.ds(start, SB_BLOCK), :]
        z = _mm_nt(q, k)
        log_beta = _log_sigmoid(z)
        log_fail = log_beta - z
        if mask is not None:
            log_fail = jnp.where(mask, log_fail, 0.0)
        hi, lo = _split(log_fail)
        sums = jnp.dot(jnp.concatenate([hi, lo], axis=1), uo, preferred_element_type=F32)
        wts = jnp.exp(log_beta + sums[:, :SB_BLOCK] + surv)
        if mask is not None:
            wts = jnp.where(mask, wts, 0.0)
        return acc + _mm(wts, v), surv + sums[:, SB_BLOCK:]

    zeros = jnp.zeros((SB_BLOCK, HEAD), F32)
    acc, surv = key_block(i, zeros, zeros, earlier)
    acc, _ = lax.fori_loop(0, i, lambda n, c: key_block(i - 1 - n, c[0], c[1], None), (acc, surv))
    o_ref[...] = acc


def _stick_breaking(qkv, batch, seq):
    n = qkv.shape[0]
    nh = SB_HEADS
    nq = seq // SB_BLOCK
    kk = lax.broadcasted_iota(jnp.int32, (SB_BLOCK, 2 * SB_BLOCK), 0)
    cc = lax.broadcasted_iota(jnp.int32, (SB_BLOCK, 2 * SB_BLOCK), 1)
    uo = jnp.where((kk > cc) | (cc >= SB_BLOCK), 1.0, 0.0).astype(BF16)
    uo = jnp.concatenate([uo, uo], axis=0)
    return pl.pallas_call(
        _sb_kernel,
        grid=(batch, nh, nq),
        in_specs=[
            pl.BlockSpec((SB_BLOCK, HEAD), lambda b, h, i: (b * nq + i, h)),
            pl.BlockSpec((seq, HEAD), lambda b, h, i: (b, nh + h)),
            pl.BlockSpec((seq, HEAD), lambda b, h, i: (b, 2 * nh + h)),
            pl.BlockSpec(uo.shape, lambda b, h, i: (0, 0)),
        ],
        out_specs=pl.BlockSpec((SB_BLOCK, HEAD), lambda b, h, i: (b * nq + i, h)),
        out_shape=jax.ShapeDtypeStruct((n, nh * HEAD), F32),
        compiler_params=pltpu.CompilerParams(
            dimension_semantics=("parallel", "parallel", "arbitrary"), vmem_limit_bytes=VMEM_LIMIT),
        name="stick_breaking",
    )(qkv, qkv, qkv, uo)


def _hgrn_kernel(q_ref, f_ref, i_ref, g_ref, lbp_ref, onw_ref, o_ref, st_ref, *, layer):
    c = pl.program_id(2)

    @pl.when(c == 0)
    def _():
        st_ref[...] = jnp.zeros_like(st_ref)

    lbp = lbp_ref[...]
    e = jnp.exp(lbp - jnp.max(lbp, axis=0, keepdims=True))
    soft = e / jnp.sum(e, axis=0, keepdims=True)
    lb = soft[0:1, :]
    for l in range(1, layer + 1):
        lb = lb + soft[l:l + 1, :]
    lb = lb - soft[0:1, :]

    ff = f_ref[...]
    a = jnp.log(lb)
    b = jnp.log1p(-lb) + _log_sigmoid(ff)
    log_f = jnp.maximum(a, b) + jnp.log1p(jnp.exp(-jnp.abs(a - b)))
    kk = (1.0 - lb) * jax.nn.sigmoid(-ff)
    qq = _silu(q_ref[...])
    vv = i_ref[...]
    gate = g_ref[...]
    gc = _cumsum_rows(log_f)
    onw = onw_ref[...]

    row = lax.broadcasted_iota(jnp.int32, (CHUNK, CHUNK), 0)
    col = lax.broadcasted_iota(jnp.int32, (CHUNK, CHUNK), 1)
    rrow = lax.broadcasted_iota(jnp.int32, (CHUNK, HEAD), 0)

    for h in range(q_ref.shape[1] // HEAD):
        lanes = slice(h * HEAD, (h + 1) * HEAD)
        q, k, v, g = qq[:, lanes], kk[:, lanes], vv[:, lanes], gc[:, lanes]

        a = jnp.where(row == col, jnp.sum(q * k, axis=1, keepdims=True), 0.0)
        s = CHUNK
        while s > SUBLANES:
            half = s // 2
            ref = jnp.concatenate(
                [jnp.broadcast_to(g[b0 + half:b0 + half + 1, :], (s, HEAD)) for b0 in range(0, CHUNK, s)],
                axis=0)
            upper = (rrow & (s - 1)) >= half
            qt = jnp.where(upper, q * jnp.exp(jnp.where(upper, g - ref, 0.0)), 0.0)
            kt = jnp.where(upper, 0.0, k * jnp.exp(jnp.where(upper, 0.0, ref - g)))
            same = (row & -s) == (col & -s)
            a = a + jnp.where(same, _mm_nt(qt, kt), 0.0)
            s = half
        for d in range(1, SUBLANES):
            valid = (rrow & (SUBLANES - 1)) >= d
            ks = pltpu.roll(k, d, 0)
            gs = pltpu.roll(g, d, 0)
            prod = jnp.where(valid, q * ks * jnp.exp(jnp.where(valid, g - gs, 0.0)), 0.0)
            a = a + jnp.where(col == row - d, jnp.sum(prod, axis=1, keepdims=True), 0.0)

        glast = g[CHUNK - 1:CHUNK, :]
        st = st_ref[h]
        o = _mm_nt(q * jnp.exp(g), st) + _mm(a, v)
        st_ref[h] = st * jnp.exp(glast) + _mm_tn(v, k * jnp.exp(glast - g))
        o_ref[:, lanes] = _rms(o, onw) * _silu(gate[:, lanes])


def _hgrn(proj, lower_bounds, layer, out_norm, batch, seq):
    n = proj.shape[0]
    w = proj.shape[1] // 4
    gw = HGRN_GROUP * HEAD
    ng = w // gw
    nc = seq // CHUNK
    rows = lambda b, c: b * nc + c
    return pl.pallas_call(
        functools.partial(_hgrn_kernel, layer=layer),
        grid=(batch, ng, nc),
        in_specs=[
            pl.BlockSpec((CHUNK, gw), lambda b, g, c: (rows(b, c), g)),
            pl.BlockSpec((CHUNK, gw), lambda b, g, c: (rows(b, c), ng + g)),
            pl.BlockSpec((CHUNK, gw), lambda b, g, c: (rows(b, c), 2 * ng + g)),
            pl.BlockSpec((CHUNK, gw), lambda b, g, c: (rows(b, c), 3 * ng + g)),
            pl.BlockSpec((lower_bounds.shape[0], gw), lambda b, g, c: (0, g)),
            pl.BlockSpec((1, HEAD), lambda b, g, c: (0, 0)),
        ],
        out_specs=pl.BlockSpec((CHUNK, gw), lambda b, g, c: (rows(b, c), g)),
        out_shape=jax.ShapeDtypeStruct((n, w), F32),
        scratch_shapes=[pltpu.VMEM((HGRN_GROUP, HEAD, HEAD), F32)],
        compiler_params=pltpu.CompilerParams(
            dimension_semantics=("parallel", "parallel", "arbitrary"), vmem_limit_bytes=VMEM_LIMIT),
        name="hgrn2",
    )(proj, proj, proj, proj, lower_bounds.astype(F32), out_norm.reshape(1, HEAD))


def _even_mixer(h, norm_w, w_in, conv_w, a_log, dt_bias, out_norm, w_out, batch, seq):
    gw, sw, nh = GDN_HEADS * HEAD, SB_HEADS * HEAD, GDN_HEADS
    o1 = 4 * gw
    o2 = o1 + 2 * nh
    w_gdn = jnp.concatenate([w_in[:, :o1], w_in[:, o1:o2],
                             jnp.zeros((w_in.shape[0], HEAD - 2 * nh), w_in.dtype)], axis=1)
    p_gdn = _norm_proj(h, norm_w, w_gdn, F32, tn=3 * HEAD)
    scale = jnp.concatenate([jnp.full((sw,), HEAD ** -0.5, F32), jnp.ones((2 * sw,), F32)])
    p_sb = _norm_proj(h, norm_w, w_in[:, o2:], BF16, tn=4 * HEAD, col_scale=scale)
    o_a = _gdn(p_gdn, conv_w, a_log, dt_bias, out_norm, batch, seq)
    o_b = _stick_breaking(p_sb, batch, seq)
    return _out_proj(h, [o_a, o_b], [w_out[:gw], w_out[gw:]])


def _odd_mixer(h, norm_w, w_in, lower_bounds, layer, out_norm, w_out, batch, seq):
    proj = _norm_proj(h, norm_w, w_in, F32, tn=4 * HEAD)
    o = _hgrn(proj, lower_bounds, layer, out_norm, batch, seq)
    return _out_proj(h, [o], [w_out])


def kernel(x, ffn1_norm, ffn1_w_gate, ffn1_w_up, ffn1_w_down, mix_norm, ffn2_norm, ffn2_w_gate, ffn2_w_up, ffn2_w_down, even_w_in, gdn_conv_w, gdn_a_log, gdn_dt_bias, gdn_out_norm, even_w_out, odd_w_in, hgrn_lower_bounds, hgrn_out_norm, odd_w_out, final_norm):
    batch, seq, d = x.shape
    depth = ffn1_norm.shape[0]
    h = x.reshape(batch * seq, d)
    for layer in range(depth):
        h = _ffn(h, ffn1_norm[layer], ffn1_w_gate[layer], ffn1_w_up[layer], ffn1_w_down[layer])
        m = layer // 2
        if layer % 2 == 0:
            h = _even_mixer(h, mix_norm[layer], even_w_in[m], gdn_conv_w[m], gdn_a_log[m], gdn_dt_bias[m],
                            gdn_out_norm[m], even_w_out[m], batch, seq)
        else:
            h = _odd_mixer(h, mix_norm[layer], odd_w_in[m], hgrn_lower_bounds, layer, hgrn_out_norm[m],
                           odd_w_out[m], batch, seq)
        h = _ffn(h, ffn2_norm[layer], ffn2_w_gate[layer], ffn2_w_up[layer], ffn2_w_down[layer],
                 final_w=final_norm if layer == depth - 1 else None)
    return h.reshape(batch, seq, d)
```

```python
import functools
import math

import jax
import jax.numpy as jnp
from jax import lax
from jax.experimental import pallas as pl
from jax.experimental.pallas import tpu as pltpu

F32 = jnp.float32
BF16 = jnp.bfloat16
EPS = 1e-6

HEAD = 128
GDN_HEADS = 8
SB_HEADS = 8
CHUNK = 64
SB_BLOCK = 128
SB_TILES = 4
SURV_FLOOR = -105.0
SUBLANES = 8

FFN_ROWS = 512
FFN_COLS = 512
PROJ_ROWS = 1024
OUT_ROWS = 256
HGRN_GROUP = 4
VMEM_LIMIT = 48 * 1024 * 1024


def _mm(a, b):
    return jnp.dot(a.astype(BF16), b.astype(BF16), preferred_element_type=F32)


def _mm_nt(a, b):
    return lax.dot_general(a.astype(BF16), b.astype(BF16), (((1,), (1,)), ((), ())),
                           preferred_element_type=F32)


def _mm_tn(a, b):
    return lax.dot_general(a.astype(BF16), b.astype(BF16), (((0,), (0,)), ((), ())),
                           preferred_element_type=F32)


def _split(x):
    hi = x.astype(BF16)
    lo = (x - hi.astype(F32)).astype(BF16)
    return hi, lo


def _mm_hi(a, b):
    ah, al = _split(a)
    bh, bl = _split(b)
    dot = functools.partial(jnp.dot, preferred_element_type=F32)
    return dot(ah, bh) + (dot(ah, bl) + dot(al, bh))


def _rms(x, w):
    return x * lax.rsqrt(jnp.mean(x * x, axis=-1, keepdims=True) + EPS) * w


def _silu(x):
    return x * jax.nn.sigmoid(x)


def _softplus(x):
    return jnp.maximum(x, 0.0) + jnp.log1p(jnp.exp(-jnp.abs(x)))


def _log_sigmoid(x):
    return jnp.minimum(x, 0.0) - jnp.log1p(jnp.exp(-jnp.abs(x)))


def _cumsum_rows(x):
    row = lax.broadcasted_iota(jnp.int32, x.shape, 0)
    s = 1
    while s < x.shape[0]:
        x = x + jnp.where(row >= s, pltpu.roll(x, s, 0), 0.0)
        s *= 2
    return x


def _shift_rows(x, prev, s):
    y = pltpu.roll(x, s, 0)
    p = pltpu.roll(prev, s, 0)
    row = lax.broadcasted_iota(jnp.int32, prev.shape, 0)
    top = jnp.where(row < s, p, y[:SUBLANES])
    return jnp.concatenate([top, y[SUBLANES:]], axis=0)


def _ffn_kernel(x_ref, nw_ref, wg_ref, wu_ref, wd_ref, fw_ref, o_ref, xn_ref, *, final_norm):
    j = pl.program_id(1)

    @pl.when(j == 0)
    def _():
        xn_ref[...] = _rms(x_ref[...], nw_ref[...]).astype(BF16)
        o_ref[...] = jnp.zeros_like(o_ref)

    xn = xn_ref[...]
    g = jnp.dot(xn, wg_ref[...], preferred_element_type=F32)
    u = jnp.dot(xn, wu_ref[...], preferred_element_type=F32)
    o_ref[...] += _mm(_silu(g) * u, wd_ref[...])

    @pl.when(j == pl.num_programs(1) - 1)
    def _():
        y = x_ref[...] + 0.5 * o_ref[...]
        if final_norm:
            y = _rms(y, fw_ref[...])
        o_ref[...] = y


def _ffn(h, norm_w, w_gate, w_up, w_down, final_w=None):
    n, d = h.shape
    f = w_gate.shape[1]
    tm, tf = min(FFN_ROWS, n), FFN_COLS
    fw = norm_w if final_w is None else final_w
    return pl.pallas_call(
        functools.partial(_ffn_kernel, final_norm=final_w is not None),
        grid=(n // tm, f // tf),
        in_specs=[
            pl.BlockSpec((tm, d), lambda i, j: (i, 0)),
            pl.BlockSpec((1, d), lambda i, j: (0, 0)),
            pl.BlockSpec((d, tf), lambda i, j: (0, j)),
            pl.BlockSpec((d, tf), lambda i, j: (0, j)),
            pl.BlockSpec((tf, d), lambda i, j: (j, 0)),
            pl.BlockSpec((1, d), lambda i, j: (0, 0)),
        ],
        out_specs=pl.BlockSpec((tm, d), lambda i, j: (i, 0)),
        out_shape=jax.ShapeDtypeStruct((n, d), F32),
        scratch_shapes=[pltpu.VMEM((tm, d), BF16)],
        compiler_params=pltpu.CompilerParams(
            dimension_semantics=("parallel", "arbitrary"), vmem_limit_bytes=VMEM_LIMIT),
        name="ffn",
    )(h, norm_w.reshape(1, d), w_gate.astype(BF16), w_up.astype(BF16), w_down.astype(BF16),
      fw.reshape(1, d))


def _norm_proj_kernel(x_ref, nw_ref, w_ref, cs_ref, o_ref, xn_ref, *, scaled):
    @pl.when(pl.program_id(1) == 0)
    def _():
        xn_ref[...] = _rms(x_ref[...], nw_ref[...]).astype(BF16)

    y = jnp.dot(xn_ref[...], w_ref[...], preferred_element_type=F32)
    if scaled:
        y = y * cs_ref[...]
    o_ref[...] = y.astype(o_ref.dtype)


def _norm_proj(h, norm_w, w, out_dtype, tn, col_scale=None):
    n, d = h.shape
    c = w.shape[1]
    tm = min(PROJ_ROWS, n)
    cs = jnp.ones((1, c), F32) if col_scale is None else col_scale.reshape(1, c)
    return pl.pallas_call(
        functools.partial(_norm_proj_kernel, scaled=col_scale is not None),
        grid=(n // tm, c // tn),
        in_specs=[
            pl.BlockSpec((tm, d), lambda i, j: (i, 0)),
            pl.BlockSpec((1, d), lambda i, j: (0, 0)),
            pl.BlockSpec((d, tn), lambda i, j: (0, j)),
            pl.BlockSpec((1, tn), lambda i, j: (0, j)),
        ],
        out_specs=pl.BlockSpec((tm, tn), lambda i, j: (i, j)),
        out_shape=jax.ShapeDtypeStruct((n, c), out_dtype),
        scratch_shapes=[pltpu.VMEM((tm, d), BF16)],
        compiler_params=pltpu.CompilerParams(
            dimension_semantics=("parallel", "arbitrary"), vmem_limit_bytes=VMEM_LIMIT),
        name="norm_proj",
    )(h, norm_w.reshape(1, d), w.astype(BF16), cs)


def _out_proj_kernel(*refs):
    h_ref, o_ref = refs[0], refs[-1]
    k = (len(refs) - 2) // 2
    acc = h_ref[...]
    for x_ref, w_ref in zip(refs[1:1 + k], refs[1 + k:1 + 2 * k]):
        acc = acc + _mm(x_ref[...], w_ref[...])
    o_ref[...] = acc


def _out_proj(h, xs, ws):
    n, d = h.shape
    tm = min(OUT_ROWS, n)
    return pl.pallas_call(
        _out_proj_kernel,
        grid=(n // tm,),
        in_specs=([pl.BlockSpec((tm, d), lambda i: (i, 0))]
                  + [pl.BlockSpec((tm, x.shape[1]), lambda i: (i, 0)) for x in xs]
                  + [pl.BlockSpec(w.shape, lambda i: (0, 0)) for w in ws]),
        out_specs=pl.BlockSpec((tm, d), lambda i: (i, 0)),
        out_shape=jax.ShapeDtypeStruct((n, d), F32),
        compiler_params=pltpu.CompilerParams(
            dimension_semantics=("parallel",), vmem_limit_bytes=VMEM_LIMIT),
        name="out_proj",
    )(h, *xs, *[w.astype(BF16) for w in ws])


def _unit_lower_inverses(lows):
    c = lows[0].shape[0]
    row = lax.broadcasted_iota(jnp.int32, (c, c), 0)
    col = lax.broadcasted_iota(jnp.int32, (c, c), 1)
    eye = jnp.where(row == col, 1.0, 0.0)
    ps = [-low for low in lows]
    ts = [eye + p for p in ps]
    s = 2
    while s < c:
        ps = [_mm_hi(p, p) for p in ps]
        ts = [t + _mm_hi(t, p) for t, p in zip(ts, ps)]
        s *= 2
    return ts


def _gdn_kernel(q_ref, k_ref, v_ref, z_ref, ba_ref, cw_ref, alog_ref, dtb_ref, onw_ref,
                o_ref, s_ref, prev_ref):
    c = pl.program_id(1)
    nh, w = GDN_HEADS, GDN_HEADS * HEAD
    heads = range(nh)

    @pl.when(c == 0)
    def _():
        s_ref[...] = jnp.zeros_like(s_ref)
        prev_ref[...] = jnp.zeros_like(prev_ref)

    x = jnp.concatenate([q_ref[...], k_ref[...], v_ref[...]], axis=1)
    prev = prev_ref[...]
    cw = cw_ref[...]
    ntap = cw.shape[0]
    conv = None
    for j in range(ntap):
        s = ntap - 1 - j
        term = (x if s == 0 else _shift_rows(x, prev, s)) * cw[j:j + 1, :]
        conv = term if conv is None else conv + term
    prev_ref[...] = x[CHUNK - SUBLANES:, :]
    qkv = _silu(conv)

    ba = ba_ref[...]
    beta_t = jax.nn.sigmoid(ba)
    g_t = -jnp.exp(alog_ref[...]) * _softplus(ba + dtb_ref[...])
    gc_t = _cumsum_rows(g_t)
    gc_rows = gc_t.T

    row = lax.broadcasted_iota(jnp.int32, (CHUNK, CHUNK), 0)
    col = lax.broadcasted_iota(jnp.int32, (CHUNK, CHUNK), 1)
    causal = row >= col
    strict = row > col
    z = z_ref[...]
    onw = onw_ref[...]

    def l2(t):
        return t * lax.rsqrt(jnp.sum(t * t, axis=-1, keepdims=True) + EPS)

    lanes = [slice(h * HEAD, (h + 1) * HEAD) for h in heads]
    qs = [l2(qkv[:, h * HEAD:(h + 1) * HEAD]) * (HEAD ** -0.5) for h in heads]
    ks = [l2(qkv[:, w + h * HEAD:w + (h + 1) * HEAD]) for h in heads]
    vs = [qkv[:, 2 * w + h * HEAD:2 * w + (h + 1) * HEAD] for h in heads]
    betas = [beta_t[:, h:h + 1] for h in heads]
    gcols = [gc_t[:, nh + h:nh + h + 1] for h in heads]
    grows = [gc_rows[nh + h:nh + h + 1, :] for h in heads]
    glasts = [g[CHUNK - 1:CHUNK, :] for g in gcols]
    decays = [jnp.where(causal, jnp.exp(jnp.where(causal, gc - gr, 0.0)), 0.0)
              for gc, gr in zip(gcols, grows)]
    kbs = [k * b for k, b in zip(ks, betas)]
    vbs = [v * b for v, b in zip(vs, betas)]
    lows = [jnp.where(strict, _mm_nt(kb, k) * d, 0.0) for kb, k, d in zip(kbs, ks, decays)]
    tinvs = _unit_lower_inverses(lows)
    us = [_mm(t, vb) for t, vb in zip(tinvs, vbs)]
    ws = [_mm(t, kb * jnp.exp(gc)) for t, kb, gc in zip(tinvs, kbs, gcols)]
    a_qks = [_mm_nt(q, k) * d for q, k, d in zip(qs, ks, decays)]
    q_decs = [q * jnp.exp(gc) for q, gc in zip(qs, gcols)]
    k_decs = [k * jnp.exp(gl - gc) for k, gl, gc in zip(ks, glasts, gcols)]
    states = [s_ref[h] for h in heads]
    v_news = [u - _mm(wm, st) for u, wm, st in zip(us, ws, states)]
    outs = [_mm(qd, st) + _mm(a, vn) for qd, st, a, vn in zip(q_decs, states, a_qks, v_news)]
    for h in heads:
        s_ref[h] = states[h] * jnp.exp(glasts[h]) + _mm_tn(k_decs[h], v_news[h])
        o_ref[:, lanes[h]] = _rms(outs[h], onw) * _silu(z[:, lanes[h]])


def _gdn(proj, conv_w, a_log, dt_bias, out_norm, batch, seq):
    n = proj.shape[0]
    nh, w = GDN_HEADS, GDN_HEADS * HEAD
    nc = seq // CHUNK
    gate_block = 4 * w // HEAD
    pad = lambda p: jnp.zeros((1, HEAD), F32).at[0, nh:2 * nh].set(p.astype(F32))
    rows = lambda b, c: b * nc + c
    return pl.pallas_call(
        _gdn_kernel,
        grid=(batch, nc),
        in_specs=[
            pl.BlockSpec((CHUNK, w), lambda b, c: (rows(b, c), 0)),
            pl.BlockSpec((CHUNK, w), lambda b, c: (rows(b, c), 1)),
            pl.BlockSpec((CHUNK, w), lambda b, c: (rows(b, c), 2)),
            pl.BlockSpec((CHUNK, w), lambda b, c: (rows(b, c), 3)),
            pl.BlockSpec((CHUNK, HEAD), lambda b, c: (rows(b, c), gate_block)),
            pl.BlockSpec(conv_w.shape, lambda b, c: (0, 0)),
            pl.BlockSpec((1, HEAD), lambda b, c: (0, 0)),
            pl.BlockSpec((1, HEAD), lambda b, c: (0, 0)),
            pl.BlockSpec((1, HEAD), lambda b, c: (0, 0)),
        ],
        out_specs=pl.BlockSpec((CHUNK, w), lambda b, c: (rows(b, c), 0)),
        out_shape=jax.ShapeDtypeStruct((n, w), F32),
        scratch_shapes=[pltpu.VMEM((nh, HEAD, HEAD), F32), pltpu.VMEM((SUBLANES, 3 * w), F32)],
        compiler_params=pltpu.CompilerParams(
            dimension_semantics=("parallel", "arbitrary"), vmem_limit_bytes=VMEM_LIMIT),
        name="gdn",
    )(proj, proj, proj, proj, proj, conv_w, pad(a_log), pad(dt_bias), out_norm.reshape(1, HEAD))


def _sb_kernel(q_ref, k_ref, v_ref, uo_ref, o_ref, surv_ref):
    base = pl.program_id(2) * SB_TILES
    uo = uo_ref[...]
    row = lax.broadcasted_iota(jnp.int32, (SB_BLOCK, SB_BLOCK), 0)
    col = lax.broadcasted_iota(jnp.int32, (SB_BLOCK, SB_BLOCK), 1)
    earlier = col < row

    def on_diagonal(t):
        top = jnp.where(earlier, t[:SB_BLOCK], 0.0)
        return top if t.shape[0] == SB_BLOCK else jnp.concatenate([top, t[SB_BLOCK:]], axis=0)

    def visit(j, r0, diagonal):
        start = pl.multiple_of(j * SB_BLOCK, SB_BLOCK)
        k = k_ref[pl.ds(start, SB_BLOCK), :]
        v = v_ref[pl.ds(start, SB_BLOCK), :]
        z = _mm_nt(q_ref[r0:, :], k)
        log_beta = _log_sigmoid(z)
        log_fail = log_beta - z
        if diagonal:
            log_fail = on_diagonal(log_fail)
        hi, lo = _split(log_fail)
        sums = jnp.dot(jnp.concatenate([hi, lo], axis=1), uo, preferred_element_type=F32)
        surv = surv_ref[r0:, :]
        wts = jnp.exp(log_beta + sums[:, :SB_BLOCK] + surv)
        if diagonal:
            wts = on_diagonal(wts)
        o_ref[r0:, :] += _mm(wts, v)
        surv = surv + sums[:, SB_BLOCK:]
        surv_ref[r0:, :] = surv
        return surv

    o_ref[...] = jnp.zeros_like(o_ref)
    surv_ref[...] = jnp.zeros_like(surv_ref)
    for t in reversed(range(SB_TILES)):
        visit(base + t, t * SB_BLOCK, True)

    def more(c):
        j, top = c
        return jnp.logical_and(j >= 0, top > SURV_FLOOR)

    def earlier_block(c):
        j, _ = c
        return j - 1, jnp.max(visit(j, 0, False))

    lax.while_loop(more, earlier_block, (base - 1, jnp.max(surv_ref[...])))


def _stick_breaking(qkv, batch, seq):
    n = qkv.shape[0]
    nh = SB_HEADS
    tq = SB_TILES * SB_BLOCK
    nq = seq // tq
    kk = lax.broadcasted_iota(jnp.int32, (SB_BLOCK, 2 * SB_BLOCK), 0)
    cc = lax.broadcasted_iota(jnp.int32, (SB_BLOCK, 2 * SB_BLOCK), 1)
    uo = jnp.where((kk > cc) | (cc >= SB_BLOCK), 1.0, 0.0).astype(BF16)
    uo = jnp.concatenate([uo, uo], axis=0)
    return pl.pallas_call(
        _sb_kernel,
        grid=(batch, nh, nq),
        in_specs=[
            pl.BlockSpec((tq, HEAD), lambda b, h, i: (b * nq + i, h)),
            pl.BlockSpec((seq, HEAD), lambda b, h, i: (b, nh + h)),
            pl.BlockSpec((seq, HEAD), lambda b, h, i: (b, 2 * nh + h)),
            pl.BlockSpec(uo.shape, lambda b, h, i: (0, 0)),
        ],
        out_specs=pl.BlockSpec((tq, HEAD), lambda b, h, i: (b * nq + i, h)),
        out_shape=jax.ShapeDtypeStruct((n, nh * HEAD), F32),
        scratch_shapes=[pltpu.VMEM((tq, HEAD), F32)],
        compiler_params=pltpu.CompilerParams(
            dimension_semantics=("parallel", "parallel", "arbitrary"), vmem_limit_bytes=VMEM_LIMIT),
        name="stick_breaking",
    )(qkv, qkv, qkv, uo)


def _hgrn_kernel(q_ref, f_ref, i_ref, g_ref, lbp_ref, onw_ref, o_ref, st_ref, *, layer):
    c = pl.program_id(2)

    @pl.when(c == 0)
    def _():
        st_ref[...] = jnp.zeros_like(st_ref)

    lbp = lbp_ref[...]
    e = jnp.exp(lbp - jnp.max(lbp, axis=0, keepdims=True))
    soft = e / jnp.sum(e, axis=0, keepdims=True)
    lb = soft[0:1, :]
    for l in range(1, layer + 1):
        lb = lb + soft[l:l + 1, :]
    lb = lb - soft[0:1, :]

    ff = f_ref[...]
    a = jnp.log(lb)
    b = jnp.log1p(-lb) + _log_sigmoid(ff)
    log_f = jnp.maximum(a, b) + jnp.log1p(jnp.exp(-jnp.abs(a - b)))
    kk = (1.0 - lb) * jax.nn.sigmoid(-ff)
    qq = _silu(q_ref[...])
    vv = i_ref[...]
    gate = g_ref[...]
    gc = _cumsum_rows(log_f)
    onw = onw_ref[...]

    row = lax.broadcasted_iota(jnp.int32, (CHUNK, CHUNK), 0)
    col = lax.broadcasted_iota(jnp.int32, (CHUNK, CHUNK), 1)
    rrow = lax.broadcasted_iota(jnp.int32, (CHUNK, HEAD), 0)

    heads = range(q_ref.shape[1] // HEAD)
    lanes = [slice(h * HEAD, (h + 1) * HEAD) for h in heads]
    qs, ks, vs, gs = ([t[:, ln] for ln in lanes] for t in (qq, kk, vv, gc))

    scores = [jnp.where(row == col, jnp.sum(q * k, axis=1, keepdims=True), 0.0) for q, k in zip(qs, ks)]
    s = CHUNK
    while s > SUBLANES:
        half = s // 2
        upper = (rrow & (s - 1)) >= half
        same = (row & -s) == (col & -s)
        refs = [jnp.concatenate(
            [jnp.broadcast_to(g[b0 + half:b0 + half + 1, :], (s, HEAD)) for b0 in range(0, CHUNK, s)],
            axis=0) for g in gs]
        qts = [jnp.where(upper, q * jnp.exp(jnp.where(upper, g - r, 0.0)), 0.0)
               for q, g, r in zip(qs, gs, refs)]
        kts = [jnp.where(upper, 0.0, k * jnp.exp(jnp.where(upper, 0.0, r - g)))
               for k, g, r in zip(ks, gs, refs)]
        scores = [a + jnp.where(same, _mm_nt(qt, kt), 0.0) for a, qt, kt in zip(scores, qts, kts)]
        s = half
    for d in range(1, SUBLANES):
        valid = (rrow & (SUBLANES - 1)) >= d
        prods = [jnp.where(valid, q * pltpu.roll(k, d, 0)
                           * jnp.exp(jnp.where(valid, g - pltpu.roll(g, d, 0), 0.0)), 0.0)
                 for q, k, g in zip(qs, ks, gs)]
        scores = [a + jnp.where(col == row - d, jnp.sum(p, axis=1, keepdims=True), 0.0)
                  for a, p in zip(scores, prods)]

    glasts = [g[CHUNK - 1:CHUNK, :] for g in gs]
    states = [st_ref[h] for h in heads]
    outs = [_mm_nt(q * jnp.exp(g), st) + _mm(a, v) for q, g, st, a, v in zip(qs, gs, states, scores, vs)]
    for h in heads:
        st_ref[h] = (states[h] * jnp.exp(glasts[h])
                     + _mm_tn(vs[h], ks[h] * jnp.exp(glasts[h] - gs[h])))
        o_ref[:, lanes[h]] = _rms(outs[h], onw) * _silu(gate[:, lanes[h]])


def _hgrn(proj, lower_bounds, layer, out_norm, batch, seq):
    n = proj.shape[0]
    w = proj.shape[1] // 4
    gw = HGRN_GROUP * HEAD
    ng = w // gw
    nc = seq // CHUNK
    rows = lambda b, c: b * nc + c
    return pl.pallas_call(
        functools.partial(_hgrn_kernel, layer=layer),
        grid=(batch, ng, nc),
        in_specs=[
            pl.BlockSpec((CHUNK, gw), lambda b, g, c: (rows(b, c), g)),
            pl.BlockSpec((CHUNK, gw), lambda b, g, c: (rows(b, c), ng + g)),
            pl.BlockSpec((CHUNK, gw), lambda b, g, c: (rows(b, c), 2 * ng + g)),
            pl.BlockSpec((CHUNK, gw), lambda b, g, c: (rows(b, c), 3 * ng + g)),
            pl.BlockSpec((lower_bounds.shape[0], gw), lambda b, g, c: (0, g)),
            pl.BlockSpec((1, HEAD), lambda b, g, c: (0, 0)),
        ],
        out_specs=pl.BlockSpec((CHUNK, gw), lambda b, g, c: (rows(b, c), g)),
        out_shape=jax.ShapeDtypeStruct((n, w), F32),
        scratch_shapes=[pltpu.VMEM((HGRN_GROUP, HEAD, HEAD), F32)],
        compiler_params=pltpu.CompilerParams(
            dimension_semantics=("parallel", "parallel", "arbitrary"), vmem_limit_bytes=VMEM_LIMIT),
        name="hgrn2",
    )(proj, proj, proj, proj, lower_bounds.astype(F32), out_norm.reshape(1, HEAD))


def _even_mixer(h, norm_w, w_in, conv_w, a_log, dt_bias, out_norm, w_out, batch, seq):
    gw, sw, nh = GDN_HEADS * HEAD, SB_HEADS * HEAD, GDN_HEADS
    o1 = 4 * gw
    o2 = o1 + 2 * nh
    w_gdn = jnp.concatenate([w_in[:, :o1], w_in[:, o1:o2],
                             jnp.zeros((w_in.shape[0], HEAD - 2 * nh), w_in.dtype)], axis=1)
    p_gdn = _norm_proj(h, norm_w, w_gdn, F32, tn=11 * HEAD)
    scale = jnp.concatenate([jnp.full((sw,), HEAD ** -0.5, F32), jnp.ones((2 * sw,), F32)])
    p_sb = _norm_proj(h, norm_w, w_in[:, o2:], BF16, tn=8 * HEAD, col_scale=scale)
    o_a = _gdn(p_gdn, conv_w, a_log, dt_bias, out_norm, batch, seq)
    o_b = _stick_breaking(p_sb, batch, seq)
    return _out_proj(h, [o_a, o_b], [w_out[:gw], w_out[gw:]])


def _odd_mixer(h, norm_w, w_in, lower_bounds, layer, out_norm, w_out, batch, seq):
    proj = _norm_proj(h, norm_w, w_in, F32, tn=8 * HEAD)
    o = _hgrn(proj, lower_bounds, layer, out_norm, batch, seq)
    return _out_proj(h, [o], [w_out])


def kernel(x, ffn1_norm, ffn1_w_gate, ffn1_w_up, ffn1_w_down, mix_norm, ffn2_norm, ffn2_w_gate, ffn2_w_up, ffn2_w_down, even_w_in, gdn_conv_w, gdn_a_log, gdn_dt_bias, gdn_out_norm, even_w_out, odd_w_in, hgrn_lower_bounds, hgrn_out_norm, odd_w_out, final_norm):
    batch, seq, d = x.shape
    depth = ffn1_norm.shape[0]
    h = x.reshape(batch * seq, d)
    for layer in range(depth):
        h = _ffn(h, ffn1_norm[layer], ffn1_w_gate[layer], ffn1_w_up[layer], ffn1_w_down[layer])
        m = layer // 2
        if layer % 2 == 0:
            h = _even_mixer(h, mix_norm[layer], even_w_in[m], gdn_conv_w[m], gdn_a_log[m], gdn_dt_bias[m],
                            gdn_out_norm[m], even_w_out[m], batch, seq)
        else:
            h = _odd_mixer(h, mix_norm[layer], odd_w_in[m], hgrn_lower_bounds, layer, hgrn_out_norm[m],
                           odd_w_out[m], batch, seq)
        h = _ffn(h, ffn2_norm[layer], ffn2_w_gate[layer], ffn2_w_up[layer], ffn2_w_down[layer],
                 final_w=final_norm if layer == depth - 1 else None)
    return h.reshape(batch, seq, d)
```

```python
import functools
import math

import jax
import jax.numpy as jnp
from jax import lax
from jax.experimental import pallas as pl
from jax.experimental.pallas import tpu as pltpu

F32 = jnp.float32
BF16 = jnp.bfloat16
EPS = 1e-6

HEAD = 128
GDN_HEADS = 8
SB_HEADS = 8
CHUNK = 64
SB_BLOCK = 128
SB_TILES = 4
SURV_FLOOR = -105.0
SUBLANES = 8

FFN_ROWS = 512
FFN_COLS = 512
FFN_VMEM_LIMIT = 48 * 1024 * 1024
PROJ_ROWS = 1024
OUT_ROWS = 256
HGRN_GROUP = 8
VMEM_LIMIT = 48 * 1024 * 1024


def _mm(a, b):
    return jnp.dot(a.astype(BF16), b.astype(BF16), preferred_element_type=F32)


def _mm_nt(a, b):
    return lax.dot_general(a.astype(BF16), b.astype(BF16), (((1,), (1,)), ((), ())),
                           preferred_element_type=F32)


def _mm_tn(a, b):
    return lax.dot_general(a.astype(BF16), b.astype(BF16), (((0,), (0,)), ((), ())),
                           preferred_element_type=F32)


def _split(x):
    hi = x.astype(BF16)
    lo = (x - hi.astype(F32)).astype(BF16)
    return hi, lo


def _mm_hi(a, b):
    ah, al = _split(a)
    bh, bl = _split(b)
    dot = functools.partial(jnp.dot, preferred_element_type=F32)
    return dot(ah, bh) + (dot(ah, bl) + dot(al, bh))


def _rms(x, w):
    return x * lax.rsqrt(jnp.mean(x * x, axis=-1, keepdims=True) + EPS) * w


def _silu(x):
    return x * jax.nn.sigmoid(x)


def _log1p_exp(x):
    return jnp.log(1.0 + jnp.exp(-jnp.abs(x)))


def _softplus(x):
    return jnp.maximum(x, 0.0) + _log1p_exp(x)


def _log_sigmoid(x):
    return jnp.minimum(x, 0.0) - _log1p_exp(x)


def _cumsum_rows(x):
    row = lax.broadcasted_iota(jnp.int32, x.shape, 0)
    s = 1
    while s < x.shape[0]:
        x = x + jnp.where(row >= s, pltpu.roll(x, s, 0), 0.0)
        s *= 2
    return x


def _shift_rows(x, prev, s):
    y = pltpu.roll(x, s, 0)
    p = pltpu.roll(prev, s, 0)
    row = lax.broadcasted_iota(jnp.int32, prev.shape, 0)
    top = jnp.where(row < s, p, y[:SUBLANES])
    return jnp.concatenate([top, y[SUBLANES:]], axis=0)


def _ffn_kernel(x_ref, nw_ref, wg_ref, wu_ref, wd_ref, fw_ref, o_ref, xn_ref, *, final_norm):
    j = pl.program_id(1)

    @pl.when(j == 0)
    def _():
        xn_ref[...] = _rms(x_ref[...], nw_ref[...]).astype(BF16)
        o_ref[...] = jnp.zeros_like(o_ref)

    xn = xn_ref[...]
    g = jnp.dot(xn, wg_ref[...], preferred_element_type=F32)
    u = jnp.dot(xn, wu_ref[...], preferred_element_type=F32)
    o_ref[...] += _mm(_silu(g) * u, wd_ref[...])

    @pl.when(j == pl.num_programs(1) - 1)
    def _():
        y = x_ref[...] + 0.5 * o_ref[...]
        if final_norm:
            y = _rms(y, fw_ref[...])
        o_ref[...] = y


def _ffn(h, norm_w, w_gate, w_up, w_down, final_w=None):
    n, d = h.shape
    f = w_gate.shape[1]
    tm, tf = min(FFN_ROWS, n), FFN_COLS
    fw = norm_w if final_w is None else final_w
    return pl.pallas_call(
        functools.partial(_ffn_kernel, final_norm=final_w is not None),
        grid=(n // tm, f // tf),
        in_specs=[
            pl.BlockSpec((tm, d), lambda i, j: (i, 0)),
            pl.BlockSpec((1, d), lambda i, j: (0, 0)),
            pl.BlockSpec((d, tf), lambda i, j: (0, j)),
            pl.BlockSpec((d, tf), lambda i, j: (0, j)),
            pl.BlockSpec((tf, d), lambda i, j: (j, 0)),
            pl.BlockSpec((1, d), lambda i, j: (0, 0)),
        ],
        out_specs=pl.BlockSpec((tm, d), lambda i, j: (i, 0)),
        out_shape=jax.ShapeDtypeStruct((n, d), F32),
        scratch_shapes=[pltpu.VMEM((tm, d), BF16)],
        compiler_params=pltpu.CompilerParams(
            dimension_semantics=("parallel", "arbitrary"), vmem_limit_bytes=FFN_VMEM_LIMIT),
        name="ffn",
    )(h, norm_w.reshape(1, d), w_gate.astype(BF16), w_up.astype(BF16), w_down.astype(BF16),
      fw.reshape(1, d))


def _norm_proj_kernel(x_ref, nw_ref, w_ref, cs_ref, o_ref, xn_ref, *, scaled):
    @pl.when(pl.program_id(1) == 0)
    def _():
        xn_ref[...] = _rms(x_ref[...], nw_ref[...]).astype(BF16)

    y = jnp.dot(xn_ref[...], w_ref[...], preferred_element_type=F32)
    if scaled:
        y = y * cs_ref[...]
    o_ref[...] = y.astype(o_ref.dtype)


def _norm_proj(h, norm_w, w, out_dtype, tn, col_scale=None):
    n, d = h.shape
    c = w.shape[1]
    tm = min(PROJ_ROWS, n)
    cs = jnp.ones((1, c), F32) if col_scale is None else col_scale.reshape(1, c)
    return pl.pallas_call(
        functools.partial(_norm_proj_kernel, scaled=col_scale is not None),
        grid=(n // tm, c // tn),
        in_specs=[
            pl.BlockSpec((tm, d), lambda i, j: (i, 0)),
            pl.BlockSpec((1, d), lambda i, j: (0, 0)),
            pl.BlockSpec((d, tn), lambda i, j: (0, j)),
            pl.BlockSpec((1, tn), lambda i, j: (0, j)),
        ],
        out_specs=pl.BlockSpec((tm, tn), lambda i, j: (i, j)),
        out_shape=jax.ShapeDtypeStruct((n, c), out_dtype),
        scratch_shapes=[pltpu.VMEM((tm, d), BF16)],
        compiler_params=pltpu.CompilerParams(
            dimension_semantics=("parallel", "arbitrary"), vmem_limit_bytes=VMEM_LIMIT),
        name="norm_proj",
    )(h, norm_w.reshape(1, d), w.astype(BF16), cs)


def _out_proj_kernel(*refs):
    h_ref, o_ref = refs[0], refs[-1]
    k = (len(refs) - 2) // 2
    acc = h_ref[...]
    for x_ref, w_ref in zip(refs[1:1 + k], refs[1 + k:1 + 2 * k]):
        acc = acc + _mm(x_ref[...], w_ref[...])
    o_ref[...] = acc


def _out_proj(h, xs, ws):
    n, d = h.shape
    tm = min(OUT_ROWS, n)
    return pl.pallas_call(
        _out_proj_kernel,
        grid=(n // tm,),
        in_specs=([pl.BlockSpec((tm, d), lambda i: (i, 0))]
                  + [pl.BlockSpec((tm, x.shape[1]), lambda i: (i, 0)) for x in xs]
                  + [pl.BlockSpec(w.shape, lambda i: (0, 0)) for w in ws]),
        out_specs=pl.BlockSpec((tm, d), lambda i: (i, 0)),
        out_shape=jax.ShapeDtypeStruct((n, d), F32),
        compiler_params=pltpu.CompilerParams(
            dimension_semantics=("parallel",), vmem_limit_bytes=VMEM_LIMIT),
        name="out_proj",
    )(h, *xs, *[w.astype(BF16) for w in ws])


def _unit_lower_inverses(lows):
    c = lows[0].shape[0]
    row = lax.broadcasted_iota(jnp.int32, (c, c), 0)
    col = lax.broadcasted_iota(jnp.int32, (c, c), 1)
    eye = jnp.where(row == col, 1.0, 0.0)
    ps = [-low for low in lows]
    ts = [eye + p for p in ps]
    s = 2
    while s < c:
        ps = [_mm_hi(p, p) for p in ps]
        ts = [t + _mm_hi(t, p) for t, p in zip(ts, ps)]
        s *= 2
    return ts


def _gdn_kernel(q_ref, k_ref, v_ref, z_ref, ba_ref, cw_ref, alog_ref, dtb_ref, onw_ref,
                o_ref, s_ref, prev_ref):
    c = pl.program_id(1)
    nh, w = GDN_HEADS, GDN_HEADS * HEAD
    heads = range(nh)

    @pl.when(c == 0)
    def _():
        s_ref[...] = jnp.zeros_like(s_ref)
        prev_ref[...] = jnp.zeros_like(prev_ref)

    x = jnp.concatenate([q_ref[...], k_ref[...], v_ref[...]], axis=1)
    prev = prev_ref[...]
    cw = cw_ref[...]
    ntap = cw.shape[0]
    conv = None
    for j in range(ntap):
        s = ntap - 1 - j
        term = (x if s == 0 else _shift_rows(x, prev, s)) * cw[j:j + 1, :]
        conv = term if conv is None else conv + term
    prev_ref[...] = x[CHUNK - SUBLANES:, :]
    qkv = _silu(conv)

    ba = ba_ref[...]
    beta_t = jax.nn.sigmoid(ba)
    g_t = -jnp.exp(alog_ref[...]) * _softplus(ba + dtb_ref[...])
    gc_t = _cumsum_rows(g_t)
    gc_rows = gc_t.T

    row = lax.broadcasted_iota(jnp.int32, (CHUNK, CHUNK), 0)
    col = lax.broadcasted_iota(jnp.int32, (CHUNK, CHUNK), 1)
    causal = row >= col
    strict = row > col
    z = z_ref[...]
    onw = onw_ref[...]

    def l2(t):
        return t * lax.rsqrt(jnp.sum(t * t, axis=-1, keepdims=True) + EPS)

    lanes = [slice(h * HEAD, (h + 1) * HEAD) for h in heads]
    qs = [l2(qkv[:, h * HEAD:(h + 1) * HEAD]) * (HEAD ** -0.5) for h in heads]
    ks = [l2(qkv[:, w + h * HEAD:w + (h + 1) * HEAD]) for h in heads]
    vs = [qkv[:, 2 * w + h * HEAD:2 * w + (h + 1) * HEAD] for h in heads]
    betas = [beta_t[:, h:h + 1] for h in heads]
    gcols = [gc_t[:, nh + h:nh + h + 1] for h in heads]
    grows = [gc_rows[nh + h:nh + h + 1, :] for h in heads]
    glasts = [g[CHUNK - 1:CHUNK, :] for g in gcols]
    decays = [jnp.where(causal, jnp.exp(jnp.where(causal, gc - gr, 0.0)), 0.0)
              for gc, gr in zip(gcols, grows)]
    kbs = [k * b for k, b in zip(ks, betas)]
    vbs = [v * b for v, b in zip(vs, betas)]
    lows = [jnp.where(strict, _mm_nt(kb, k) * d, 0.0) for kb, k, d in zip(kbs, ks, decays)]
    tinvs = _unit_lower_inverses(lows)
    us = [_mm(t, vb) for t, vb in zip(tinvs, vbs)]
    ws = [_mm(t, kb * jnp.exp(gc)) for t, kb, gc in zip(tinvs, kbs, gcols)]
    a_qks = [_mm_nt(q, k) * d for q, k, d in zip(qs, ks, decays)]
    q_decs = [q * jnp.exp(gc) for q, gc in zip(qs, gcols)]
    k_decs = [k * jnp.exp(gl - gc) for k, gl, gc in zip(ks, glasts, gcols)]
    states = [s_ref[h] for h in heads]
    v_news = [u - _mm(wm, st) for u, wm, st in zip(us, ws, states)]
    outs = [_mm(qd, st) + _mm(a, vn) for qd, st, a, vn in zip(q_decs, states, a_qks, v_news)]
    for h in heads:
        s_ref[h] = states[h] * jnp.exp(glasts[h]) + _mm_tn(k_decs[h], v_news[h])
        o_ref[:, lanes[h]] = _rms(outs[h], onw) * _silu(z[:, lanes[h]])


def _gdn(proj, conv_w, a_log, dt_bias, out_norm, batch, seq):
    n = proj.shape[0]
    nh, w = GDN_HEADS, GDN_HEADS * HEAD
    nc = seq // CHUNK
    gate_block = 4 * w // HEAD
    pad = lambda p: jnp.zeros((1, HEAD), F32).at[0, nh:2 * nh].set(p.astype(F32))
    rows = lambda b, c: b * nc + c
    return pl.pallas_call(
        _gdn_kernel,
        grid=(batch, nc),
        in_specs=[
            pl.BlockSpec((CHUNK, w), lambda b, c: (rows(b, c), 0)),
            pl.BlockSpec((CHUNK, w), lambda b, c: (rows(b, c), 1)),
            pl.BlockSpec((CHUNK, w), lambda b, c: (rows(b, c), 2)),
            pl.BlockSpec((CHUNK, w), lambda b, c: (rows(b, c), 3)),
            pl.BlockSpec((CHUNK, HEAD), lambda b, c: (rows(b, c), gate_block)),
            pl.BlockSpec(conv_w.shape, lambda b, c: (0, 0)),
            pl.BlockSpec((1, HEAD), lambda b, c: (0, 0)),
            pl.BlockSpec((1, HEAD), lambda b, c: (0, 0)),
            pl.BlockSpec((1, HEAD), lambda b, c: (0, 0)),
        ],
        out_specs=pl.BlockSpec((CHUNK, w), lambda b, c: (rows(b, c), 0)),
        out_shape=jax.ShapeDtypeStruct((n, w), F32),
        scratch_shapes=[pltpu.VMEM((nh, HEAD, HEAD), F32), pltpu.VMEM((SUBLANES, 3 * w), F32)],
        compiler_params=pltpu.CompilerParams(
            dimension_semantics=("parallel", "arbitrary"), vmem_limit_bytes=VMEM_LIMIT),
        name="gdn",
    )(proj, proj, proj, proj, proj, conv_w, pad(a_log), pad(dt_bias), out_norm.reshape(1, HEAD))


def _sb_kernel(q_ref, k_ref, v_ref, uo_ref, o_ref, surv_ref):
    base = pl.program_id(2) * SB_TILES
    uo = uo_ref[...]
    row = lax.broadcasted_iota(jnp.int32, (SB_BLOCK, SB_BLOCK), 0)
    col = lax.broadcasted_iota(jnp.int32, (SB_BLOCK, SB_BLOCK), 1)
    earlier = col < row

    def on_diagonal(t):
        top = jnp.where(earlier, t[:SB_BLOCK], 0.0)
        return top if t.shape[0] == SB_BLOCK else jnp.concatenate([top, t[SB_BLOCK:]], axis=0)

    def block(j):
        return pl.ds(pl.multiple_of(j * SB_BLOCK, SB_BLOCK), SB_BLOCK)

    def score(j, r0, diagonal):
        z = _mm_nt(q_ref[r0:, :], k_ref[block(j), :])
        log_beta = _log_sigmoid(z)
        log_fail = log_beta - z
        if diagonal:
            log_fail = on_diagonal(log_fail)
        hi, lo = _split(log_fail)
        sums = jnp.dot(jnp.concatenate([hi, lo], axis=1), uo, preferred_element_type=F32)
        return log_beta + sums[:, :SB_BLOCK], sums[:, SB_BLOCK:]

    def absorb(j, r0, diagonal, log_w, total):
        surv = surv_ref[r0:, :]
        wts = jnp.exp(log_w + surv)
        if diagonal:
            wts = on_diagonal(wts)
        o_ref[r0:, :] += _mm(wts, v_ref[block(j), :])
        surv = surv + total
        surv_ref[r0:, :] = surv
        return surv

    o_ref[...] = jnp.zeros_like(o_ref)
    surv_ref[...] = jnp.zeros_like(surv_ref)
    own = [(base + t, t * SB_BLOCK) for t in reversed(range(SB_TILES))]
    scored = [score(j, r0, True) for j, r0 in own]
    for (j, r0), (log_w, total) in zip(own, scored):
        absorb(j, r0, True, log_w, total)

    def more(c):
        j, top = c
        return jnp.logical_and(j >= 0, top > SURV_FLOOR)

    def earlier_blocks(c):
        j, _ = c
        has_second = j >= 1
        j2 = jnp.maximum(j - 1, 0)
        log_w1, total1 = score(j, 0, False)
        log_w2, total2 = score(j2, 0, False)
        log_w2 = jnp.where(has_second, log_w2, -jnp.inf)
        total2 = jnp.where(has_second, total2, 0.0)
        absorb(j, 0, False, log_w1, total1)
        return j - 2, jnp.max(absorb(j2, 0, False, log_w2, total2))

    lax.while_loop(more, earlier_blocks, (base - 1, jnp.max(surv_ref[...])))


def _stick_breaking(qkv, batch, seq):
    n = qkv.shape[0]
    nh = SB_HEADS
    tq = SB_TILES * SB_BLOCK
    nq = seq // tq
    kk = lax.broadcasted_iota(jnp.int32, (SB_BLOCK, 2 * SB_BLOCK), 0)
    cc = lax.broadcasted_iota(jnp.int32, (SB_BLOCK, 2 * SB_BLOCK), 1)
    uo = jnp.where((kk > cc) | (cc >= SB_BLOCK), 1.0, 0.0).astype(BF16)
    uo = jnp.concatenate([uo, uo], axis=0)
    return pl.pallas_call(
        _sb_kernel,
        grid=(batch, nh, nq),
        in_specs=[
            pl.BlockSpec((tq, HEAD), lambda b, h, i: (b * nq + i, h)),
            pl.BlockSpec((seq, HEAD), lambda b, h, i: (b, nh + h)),
            pl.BlockSpec((seq, HEAD), lambda b, h, i: (b, 2 * nh + h)),
            pl.BlockSpec(uo.shape, lambda b, h, i: (0, 0)),
        ],
        out_specs=pl.BlockSpec((tq, HEAD), lambda b, h, i: (b * nq + i, h)),
        out_shape=jax.ShapeDtypeStruct((n, nh * HEAD), F32),
        scratch_shapes=[pltpu.VMEM((tq, HEAD), F32)],
        compiler_params=pltpu.CompilerParams(
            dimension_semantics=("parallel", "parallel", "arbitrary"), vmem_limit_bytes=VMEM_LIMIT),
        name="stick_breaking",
    )(qkv, qkv, qkv, uo)


def _hgrn_kernel(q_ref, f_ref, i_ref, g_ref, lbp_ref, onw_ref, sel_ref, o_ref, st_ref, *, layer):
    c = pl.program_id(2)

    @pl.when(c == 0)
    def _():
        st_ref[...] = jnp.zeros_like(st_ref)

    lbp = lbp_ref[...]
    e = jnp.exp(lbp - jnp.max(lbp, axis=0, keepdims=True))
    soft = e / jnp.sum(e, axis=0, keepdims=True)
    lb = soft[0:1, :]
    for l in range(1, layer + 1):
        lb = lb + soft[l:l + 1, :]
    lb = lb - soft[0:1, :]

    ff = f_ref[...]
    t = jnp.exp(-jnp.abs(ff))
    a = jnp.log(lb)
    b = jnp.log(1.0 - lb) + (jnp.minimum(ff, 0.0) - jnp.log(1.0 + t))
    log_f = jnp.maximum(a, b) + _log1p_exp(a - b)
    kk = (1.0 - lb) * (jnp.where(ff >= 0.0, t, 1.0) / (1.0 + t))
    qq = _silu(q_ref[...])
    vv = i_ref[...]
    gate = g_ref[...]
    onw = onw_ref[...]

    hi, lo = _split(log_f)
    runs = jnp.dot(sel_ref[...], jnp.concatenate([hi, lo], axis=0), preferred_element_type=F32)
    gc = runs[:CHUNK]
    q_dec = (qq * jnp.exp(gc)).astype(BF16)
    k_dec = (kk * jnp.exp(runs[CHUNK:2 * CHUNK])).astype(BF16)
    g_last = jnp.exp(gc[CHUNK - 1:CHUNK, :])

    row = lax.broadcasted_iota(jnp.int32, (CHUNK, CHUNK), 0)
    col = lax.broadcasted_iota(jnp.int32, (CHUNK, CHUNK), 1)
    rrow = lax.broadcasted_iota(jnp.int32, qq.shape, 0)
    heads = range(q_ref.shape[1] // HEAD)
    lanes = [slice(h * HEAD, (h + 1) * HEAD) for h in heads]

    scores = [jnp.where(row == col, jnp.sum(qq[:, ln] * kk[:, ln], axis=1, keepdims=True), 0.0)
              for ln in lanes]
    s, level = CHUNK, 2
    while s > 1:
        upper = (rrow & (s - 1)) >= s // 2
        same = (row & -s) == (col & -s)
        factor = jnp.exp(runs[level * CHUNK:(level + 1) * CHUNK])
        qt = jnp.where(upper, qq * factor, 0.0).astype(BF16)
        kt = jnp.where(upper, 0.0, kk * factor).astype(BF16)
        scores = [a + jnp.where(same, _mm_nt(qt[:, ln], kt[:, ln]), 0.0) for a, ln in zip(scores, lanes)]
        s, level = s // 2, level + 1

    states = [st_ref[h] for h in heads]
    outs = [_mm_nt(q_dec[:, ln], st) + _mm(a, vv[:, ln]) for ln, st, a in zip(lanes, states, scores)]
    for h in heads:
        ln = lanes[h]
        st_ref[h] = states[h] * g_last[:, ln] + _mm_tn(vv[:, ln], k_dec[:, ln])
        o_ref[:, ln] = _rms(outs[h], onw) * _silu(gate[:, ln])


def _run_selectors():
    i = lax.broadcasted_iota(jnp.int32, (CHUNK, CHUNK), 0)
    r = lax.broadcasted_iota(jnp.int32, (CHUNK, CHUNK), 1)
    blocks = [r <= i, r > i]
    s = CHUNK
    while s > 1:
        mid = (i & -s) + s // 2
        blocks.append(jnp.where(i >= mid, (r > mid) & (r <= i), (r > i) & (r <= mid)))
        s //= 2
    sel = jnp.concatenate(blocks, axis=0).astype(BF16)
    return jnp.concatenate([sel, sel], axis=1)


def _hgrn(proj, lower_bounds, layer, out_norm, batch, seq):
    n = proj.shape[0]
    w = proj.shape[1] // 4
    gw = HGRN_GROUP * HEAD
    ng = w // gw
    nc = seq // CHUNK
    sel = _run_selectors()
    rows = lambda b, c: b * nc + c
    return pl.pallas_call(
        functools.partial(_hgrn_kernel, layer=layer),
        grid=(batch, ng, nc),
        in_specs=[
            pl.BlockSpec((CHUNK, gw), lambda b, g, c: (rows(b, c), g)),
            pl.BlockSpec((CHUNK, gw), lambda b, g, c: (rows(b, c), ng + g)),
            pl.BlockSpec((CHUNK, gw), lambda b, g, c: (rows(b, c), 2 * ng + g)),
            pl.BlockSpec((CHUNK, gw), lambda b, g, c: (rows(b, c), 3 * ng + g)),
            pl.BlockSpec((lower_bounds.shape[0], gw), lambda b, g, c: (0, g)),
            pl.BlockSpec((1, HEAD), lambda b, g, c: (0, 0)),
            pl.BlockSpec(sel.shape, lambda b, g, c: (0, 0)),
        ],
        out_specs=pl.BlockSpec((CHUNK, gw), lambda b, g, c: (rows(b, c), g)),
        out_shape=jax.ShapeDtypeStruct((n, w), F32),
        scratch_shapes=[pltpu.VMEM((HGRN_GROUP, HEAD, HEAD), F32)],
        compiler_params=pltpu.CompilerParams(
            dimension_semantics=("parallel", "parallel", "arbitrary"), vmem_limit_bytes=VMEM_LIMIT),
        name="hgrn2",
    )(proj, proj, proj, proj, lower_bounds.astype(F32), out_norm.reshape(1, HEAD), sel)


def _even_mixer(h, norm_w, w_in, conv_w, a_log, dt_bias, out_norm, w_out, batch, seq):
    gw, sw, nh = GDN_HEADS * HEAD, SB_HEADS * HEAD, GDN_HEADS
    o1 = 4 * gw
    o2 = o1 + 2 * nh
    w_gdn = jnp.concatenate([w_in[:, :o1], w_in[:, o1:o2],
                             jnp.zeros((w_in.shape[0], HEAD - 2 * nh), w_in.dtype)], axis=1)
    p_gdn = _norm_proj(h, norm_w, w_gdn, F32, tn=11 * HEAD)
    scale = jnp.concatenate([jnp.full((sw,), HEAD ** -0.5, F32), jnp.ones((2 * sw,), F32)])
    p_sb = _norm_proj(h, norm_w, w_in[:, o2:], BF16, tn=8 * HEAD, col_scale=scale)
    o_a = _gdn(p_gdn, conv_w, a_log, dt_bias, out_norm, batch, seq)
    o_b = _stick_breaking(p_sb, batch, seq)
    return _out_proj(h, [o_a, o_b], [w_out[:gw], w_out[gw:]])


def _odd_mixer(h, norm_w, w_in, lower_bounds, layer, out_norm, w_out, batch, seq):
    proj = _norm_proj(h, norm_w, w_in, F32, tn=8 * HEAD)
    o = _hgrn(proj, lower_bounds, layer, out_norm, batch, seq)
    return _out_proj(h, [o], [w_out])


def kernel(x, ffn1_norm, ffn1_w_gate, ffn1_w_up, ffn1_w_down, mix_norm, ffn2_norm, ffn2_w_gate, ffn2_w_up, ffn2_w_down, even_w_in, gdn_conv_w, gdn_a_log, gdn_dt_bias, gdn_out_norm, even_w_out, odd_w_in, hgrn_lower_bounds, hgrn_out_norm, odd_w_out, final_norm):
    batch, seq, d = x.shape
    depth = ffn1_norm.shape[0]
    h = x.reshape(batch * seq, d)
    for layer in range(depth):
        h = _ffn(h, ffn1_norm[layer], ffn1_w_gate[layer], ffn1_w_up[layer], ffn1_w_down[layer])
        m = layer // 2
        if layer % 2 == 0:
            h = _even_mixer(h, mix_norm[layer], even_w_in[m], gdn_conv_w[m], gdn_a_log[m], gdn_dt_bias[m],
                            gdn_out_norm[m], even_w_out[m], batch, seq)
        else:
            h = _odd_mixer(h, mix_norm[layer], odd_w_in[m], hgrn_lower_bounds, layer, hgrn_out_norm[m],
                           odd_w_out[m], batch, seq)
        h = _ffn(h, ffn2_norm[layer], ffn2_w_gate[layer], ffn2_w_up[layer], ffn2_w_down[layer],
                 final_w=final_norm if layer == depth - 1 else None)
    return h.reshape(batch, seq, d)
```

```python
import functools
import math

import jax
import jax.numpy as jnp
from jax import lax
from jax.experimental import pallas as pl
from jax.experimental.pallas import tpu as pltpu

F32 = jnp.float32
BF16 = jnp.bfloat16
EPS = 1e-6

HEAD = 128
GDN_HEADS = 8
SB_HEADS = 8
CHUNK = 64
GDN_CHUNKS = 2
SB_BLOCK = 128
SB_TILES = 4
SURV_FLOOR = -105.0
SUBLANES = 8

FFN_ROWS = 512
FFN_COLS = 512
FFN_VMEM_LIMIT = 48 * 1024 * 1024
PROJ_ROWS = 1024
OUT_ROWS = 256
HGRN_GROUP = 8
HGRN_CHUNKS = 2
VMEM_LIMIT = 48 * 1024 * 1024


def _mm(a, b):
    return jnp.dot(a.astype(BF16), b.astype(BF16), preferred_element_type=F32)


def _mm_nt(a, b):
    return lax.dot_general(a.astype(BF16), b.astype(BF16), (((1,), (1,)), ((), ())),
                           preferred_element_type=F32)


def _mm_tn(a, b):
    return lax.dot_general(a.astype(BF16), b.astype(BF16), (((0,), (0,)), ((), ())),
                           preferred_element_type=F32)


def _split(x):
    hi = x.astype(BF16)
    lo = (x - hi.astype(F32)).astype(BF16)
    return hi, lo


def _mm_hi(lhs, b):
    m, n = lhs[0].shape[0], len(lhs)
    his, los = zip(*[_split(a) for a in lhs])
    bh, bl = _split(b)
    top = jnp.dot(jnp.concatenate(his + los, axis=0), bh, preferred_element_type=F32)
    low = jnp.dot(jnp.concatenate(his, axis=0), bl, preferred_element_type=F32)
    rows = lambda t, i: t[i * m:(i + 1) * m]
    return [rows(top, i) + (rows(low, i) + rows(top, n + i)) for i in range(n)]


def _rms(x, w):
    return x * lax.rsqrt(jnp.mean(x * x, axis=-1, keepdims=True) + EPS) * w


def _silu(x):
    return x * jax.nn.sigmoid(x)


def _log1p_exp(x):
    return jnp.log(1.0 + jnp.exp(-jnp.abs(x)))


def _softplus(x):
    return jnp.maximum(x, 0.0) + _log1p_exp(x)


def _log_sigmoid(x):
    return jnp.minimum(x, 0.0) - _log1p_exp(x)


def _cumsum_rows(x):
    row = lax.broadcasted_iota(jnp.int32, x.shape, 0)
    s = 1
    while s < x.shape[0]:
        x = x + jnp.where(row >= s, pltpu.roll(x, s, 0), 0.0)
        s *= 2
    return x


def _shift_rows(x, prev, s):
    y = pltpu.roll(x, s, 0)
    p = pltpu.roll(prev, s, 0)
    row = lax.broadcasted_iota(jnp.int32, prev.shape, 0)
    top = jnp.where(row < s, p, y[:SUBLANES])
    return jnp.concatenate([top, y[SUBLANES:]], axis=0)


def _ffn_kernel(x_ref, nw_ref, wg_ref, wu_ref, wd_ref, fw_ref, o_ref, xn_ref, *, final_norm):
    j = pl.program_id(1)

    @pl.when(j == 0)
    def _():
        xn_ref[...] = _rms(x_ref[...], nw_ref[...]).astype(BF16)
        o_ref[...] = jnp.zeros_like(o_ref)

    xn = xn_ref[...]
    g = jnp.dot(xn, wg_ref[...], preferred_element_type=F32)
    u = jnp.dot(xn, wu_ref[...], preferred_element_type=F32)
    o_ref[...] += _mm(_silu(g) * u, wd_ref[...])

    @pl.when(j == pl.num_programs(1) - 1)
    def _():
        y = x_ref[...] + 0.5 * o_ref[...]
        if final_norm:
            y = _rms(y, fw_ref[...])
        o_ref[...] = y


def _ffn(h, norm_w, w_gate, w_up, w_down, final_w=None):
    n, d = h.shape
    f = w_gate.shape[1]
    tm, tf = min(FFN_ROWS, n), FFN_COLS
    fw = norm_w if final_w is None else final_w
    return pl.pallas_call(
        functools.partial(_ffn_kernel, final_norm=final_w is not None),
        grid=(n // tm, f // tf),
        in_specs=[
            pl.BlockSpec((tm, d), lambda i, j: (i, 0)),
            pl.BlockSpec((1, d), lambda i, j: (0, 0)),
            pl.BlockSpec((d, tf), lambda i, j: (0, j)),
            pl.BlockSpec((d, tf), lambda i, j: (0, j)),
            pl.BlockSpec((tf, d), lambda i, j: (j, 0)),
            pl.BlockSpec((1, d), lambda i, j: (0, 0)),
        ],
        out_specs=pl.BlockSpec((tm, d), lambda i, j: (i, 0)),
        out_shape=jax.ShapeDtypeStruct((n, d), F32),
        scratch_shapes=[pltpu.VMEM((tm, d), BF16)],
        compiler_params=pltpu.CompilerParams(
            dimension_semantics=("parallel", "arbitrary"), vmem_limit_bytes=FFN_VMEM_LIMIT),
        name="ffn",
    )(h, norm_w.reshape(1, d), w_gate.astype(BF16), w_up.astype(BF16), w_down.astype(BF16),
      fw.reshape(1, d))


def _norm_proj_kernel(x_ref, nw_ref, w_ref, cs_ref, o_ref, xn_ref, *, scaled):
    @pl.when(pl.program_id(1) == 0)
    def _():
        xn_ref[...] = _rms(x_ref[...], nw_ref[...]).astype(BF16)

    y = jnp.dot(xn_ref[...], w_ref[...], preferred_element_type=F32)
    if scaled:
        y = y * cs_ref[...]
    o_ref[...] = y.astype(o_ref.dtype)


def _norm_proj(h, norm_w, w, out_dtype, tn, col_scale=None):
    n, d = h.shape
    c = w.shape[1]
    tm = min(PROJ_ROWS, n)
    cs = jnp.ones((1, c), F32) if col_scale is None else col_scale.reshape(1, c)
    return pl.pallas_call(
        functools.partial(_norm_proj_kernel, scaled=col_scale is not None),
        grid=(n // tm, c // tn),
        in_specs=[
            pl.BlockSpec((tm, d), lambda i, j: (i, 0)),
            pl.BlockSpec((1, d), lambda i, j: (0, 0)),
            pl.BlockSpec((d, tn), lambda i, j: (0, j)),
            pl.BlockSpec((1, tn), lambda i, j: (0, j)),
        ],
        out_specs=pl.BlockSpec((tm, tn), lambda i, j: (i, j)),
        out_shape=jax.ShapeDtypeStruct((n, c), out_dtype),
        scratch_shapes=[pltpu.VMEM((tm, d), BF16)],
        compiler_params=pltpu.CompilerParams(
            dimension_semantics=("parallel", "arbitrary"), vmem_limit_bytes=VMEM_LIMIT),
        name="norm_proj",
    )(h, norm_w.reshape(1, d), w.astype(BF16), cs)


def _out_proj_kernel(*refs):
    h_ref, o_ref = refs[0], refs[-1]
    k = (len(refs) - 2) // 2
    acc = h_ref[...]
    for x_ref, w_ref in zip(refs[1:1 + k], refs[1 + k:1 + 2 * k]):
        acc = acc + _mm(x_ref[...], w_ref[...])
    o_ref[...] = acc


def _out_proj(h, xs, ws):
    n, d = h.shape
    tm = min(OUT_ROWS, n)
    return pl.pallas_call(
        _out_proj_kernel,
        grid=(n // tm,),
        in_specs=([pl.BlockSpec((tm, d), lambda i: (i, 0))]
                  + [pl.BlockSpec((tm, x.shape[1]), lambda i: (i, 0)) for x in xs]
                  + [pl.BlockSpec(w.shape, lambda i: (0, 0)) for w in ws]),
        out_specs=pl.BlockSpec((tm, d), lambda i: (i, 0)),
        out_shape=jax.ShapeDtypeStruct((n, d), F32),
        compiler_params=pltpu.CompilerParams(
            dimension_semantics=("parallel",), vmem_limit_bytes=VMEM_LIMIT),
        name="out_proj",
    )(h, *xs, *[w.astype(BF16) for w in ws])


def _unit_lower_inverses(lows):
    c = lows[0].shape[0]
    row = lax.broadcasted_iota(jnp.int32, (c, c), 0)
    col = lax.broadcasted_iota(jnp.int32, (c, c), 1)
    eye = jnp.where(row == col, 1.0, 0.0)
    ps = [-low for low in lows]
    ts = [eye + p for p in ps]
    ps = [_mm_hi([p], p)[0] for p in ps]
    s = 2
    while s < c:
        last = 2 * s >= c
        prods = [_mm_hi([t] if last else [t, p], p) for t, p in zip(ts, ps)]
        ts = [t + pr[0] for t, pr in zip(ts, prods)]
        if not last:
            ps = [pr[1] for pr in prods]
        s *= 2
    return ts


def _gdn_kernel(q_ref, k_ref, v_ref, z_ref, ba_ref, cw_ref, alog_ref, dtb_ref, onw_ref,
                o_ref, s_ref, prev_ref):
    c = pl.program_id(1)
    nh, w = GDN_HEADS, GDN_HEADS * HEAD
    heads = range(nh)
    nrows = q_ref.shape[0]
    chunks = [slice(r, r + CHUNK) for r in range(0, nrows, CHUNK)]

    @pl.when(c == 0)
    def _():
        s_ref[...] = jnp.zeros_like(s_ref)
        prev_ref[...] = jnp.zeros_like(prev_ref)

    x = jnp.concatenate([q_ref[...], k_ref[...], v_ref[...]], axis=1)
    prev = prev_ref[...]
    cw = cw_ref[...]
    ntap = cw.shape[0]
    conv = None
    for j in range(ntap):
        s = ntap - 1 - j
        term = (x if s == 0 else _shift_rows(x, prev, s)) * cw[j:j + 1, :]
        conv = term if conv is None else conv + term
    prev_ref[...] = x[nrows - SUBLANES:, :]
    qkv = _silu(conv)

    ba = ba_ref[...]
    beta_t = jax.nn.sigmoid(ba)
    g_t = -jnp.exp(alog_ref[...]) * _softplus(ba + dtb_ref[...])
    gc_ts = [_cumsum_rows(g_t[rs]) for rs in chunks]
    gc_rows = [g.T for g in gc_ts]

    row = lax.broadcasted_iota(jnp.int32, (CHUNK, CHUNK), 0)
    col = lax.broadcasted_iota(jnp.int32, (CHUNK, CHUNK), 1)
    causal = row >= col
    strict = row > col
    z = z_ref[...]
    onw = onw_ref[...]

    def l2(t):
        return t * lax.rsqrt(jnp.sum(t * t, axis=-1, keepdims=True) + EPS)

    lanes = [slice(h * HEAD, (h + 1) * HEAD) for h in heads]
    qn = [l2(qkv[:, h * HEAD:(h + 1) * HEAD]) * (HEAD ** -0.5) for h in heads]
    kn = [l2(qkv[:, w + h * HEAD:w + (h + 1) * HEAD]) for h in heads]

    pairs = [(i, h) for i in range(len(chunks)) for h in heads]
    qs = [qn[h][chunks[i]] for i, h in pairs]
    ks = [kn[h][chunks[i]] for i, h in pairs]
    vs = [qkv[chunks[i], 2 * w + h * HEAD:2 * w + (h + 1) * HEAD] for i, h in pairs]
    betas = [beta_t[chunks[i], h:h + 1] for i, h in pairs]
    gcols = [gc_ts[i][:, nh + h:nh + h + 1] for i, h in pairs]
    grows = [gc_rows[i][nh + h:nh + h + 1, :] for i, h in pairs]
    glasts = [g[CHUNK - 1:CHUNK, :] for g in gcols]
    decays = [jnp.where(causal, jnp.exp(jnp.where(causal, gc - gr, 0.0)), 0.0)
              for gc, gr in zip(gcols, grows)]
    kbs = [k * b for k, b in zip(ks, betas)]
    vbs = [v * b for v, b in zip(vs, betas)]
    kq_ks = [_mm_nt(jnp.concatenate([kb, q], axis=0), k) for kb, q, k in zip(kbs, qs, ks)]
    lows = [jnp.where(strict, kq[:CHUNK] * d, 0.0) for kq, d in zip(kq_ks, decays)]
    a_qks = [kq[CHUNK:] * d for kq, d in zip(kq_ks, decays)]
    tinvs = _unit_lower_inverses(lows)
    uws = [_mm(t, jnp.concatenate([vb, kb * jnp.exp(gc)], axis=1))
           for t, vb, kb, gc in zip(tinvs, vbs, kbs, gcols)]
    us = [uw[:, :HEAD] for uw in uws]
    ws = [uw[:, HEAD:] for uw in uws]
    q_decs = [q * jnp.exp(gc) for q, gc in zip(qs, gcols)]
    k_decs = [k * jnp.exp(gl - gc) for k, gl, gc in zip(ks, glasts, gcols)]
    wqs = [jnp.concatenate([wm, qd], axis=0) for wm, qd in zip(ws, q_decs)]

    states = [s_ref[h] for h in heads]
    for i, rs in enumerate(chunks):
        ids = [i * nh + h for h in heads]
        wq_ss = [_mm(wqs[p], st) for p, st in zip(ids, states)]
        v_news = [us[p] - wq[:CHUNK] for p, wq in zip(ids, wq_ss)]
        outs = [wq[CHUNK:] + _mm(a_qks[p], vn) for p, wq, vn in zip(ids, wq_ss, v_news)]
        states = [st * jnp.exp(glasts[p]) + _mm_tn(k_decs[p], vn) for p, st, vn in zip(ids, states, v_news)]
        for h in heads:
            o_ref[rs, lanes[h]] = _rms(outs[h], onw) * _silu(z[rs, lanes[h]])
    for h in heads:
        s_ref[h] = states[h]


def _gdn(proj, conv_w, a_log, dt_bias, out_norm, batch, seq):
    n = proj.shape[0]
    nh, w = GDN_HEADS, GDN_HEADS * HEAD
    rows = GDN_CHUNKS * CHUNK
    steps = seq // rows
    gate_block = 4 * w // HEAD
    pad = lambda p: jnp.zeros((1, HEAD), F32).at[0, nh:2 * nh].set(p.astype(F32))
    at = lambda b, c: b * steps + c
    return pl.pallas_call(
        _gdn_kernel,
        grid=(batch, steps),
        in_specs=[
            pl.BlockSpec((rows, w), lambda b, c: (at(b, c), 0)),
            pl.BlockSpec((rows, w), lambda b, c: (at(b, c), 1)),
            pl.BlockSpec((rows, w), lambda b, c: (at(b, c), 2)),
            pl.BlockSpec((rows, w), lambda b, c: (at(b, c), 3)),
            pl.BlockSpec((rows, HEAD), lambda b, c: (at(b, c), gate_block)),
            pl.BlockSpec(conv_w.shape, lambda b, c: (0, 0)),
            pl.BlockSpec((1, HEAD), lambda b, c: (0, 0)),
            pl.BlockSpec((1, HEAD), lambda b, c: (0, 0)),
            pl.BlockSpec((1, HEAD), lambda b, c: (0, 0)),
        ],
        out_specs=pl.BlockSpec((rows, w), lambda b, c: (at(b, c), 0)),
        out_shape=jax.ShapeDtypeStruct((n, w), F32),
        scratch_shapes=[pltpu.VMEM((nh, HEAD, HEAD), F32), pltpu.VMEM((SUBLANES, 3 * w), F32)],
        compiler_params=pltpu.CompilerParams(
            dimension_semantics=("parallel", "arbitrary"), vmem_limit_bytes=VMEM_LIMIT),
        name="gdn",
    )(proj, proj, proj, proj, proj, conv_w, pad(a_log), pad(dt_bias), out_norm.reshape(1, HEAD))


def _sb_kernel(q_ref, k_ref, v_ref, uo_ref, o_ref, surv_ref):
    base = pl.program_id(2) * SB_TILES
    uo = uo_ref[...]
    row = lax.broadcasted_iota(jnp.int32, (SB_BLOCK, SB_BLOCK), 0)
    col = lax.broadcasted_iota(jnp.int32, (SB_BLOCK, SB_BLOCK), 1)
    earlier = col < row

    def on_diagonal(t):
        top = jnp.where(earlier, t[:SB_BLOCK], 0.0)
        return top if t.shape[0] == SB_BLOCK else jnp.concatenate([top, t[SB_BLOCK:]], axis=0)

    def block(j):
        return pl.ds(pl.multiple_of(j * SB_BLOCK, SB_BLOCK), SB_BLOCK)

    def score(j, r0, diagonal):
        z = _mm_nt(q_ref[r0:, :], k_ref[block(j), :])
        log_beta = _log_sigmoid(z)
        log_fail = log_beta - z
        if diagonal:
            log_fail = on_diagonal(log_fail)
        hi, lo = _split(log_fail)
        sums = jnp.dot(jnp.concatenate([hi, lo], axis=1), uo, preferred_element_type=F32)
        return log_beta + sums[:, :SB_BLOCK], sums[:, SB_BLOCK:]

    def absorb(j, r0, diagonal, log_w, total):
        surv = surv_ref[r0:, :]
        wts = jnp.exp(log_w + surv)
        if diagonal:
            wts = on_diagonal(wts)
        o_ref[r0:, :] += _mm(wts, v_ref[block(j), :])
        surv = surv + total
        surv_ref[r0:, :] = surv
        return surv

    o_ref[...] = jnp.zeros_like(o_ref)
    surv_ref[...] = jnp.zeros_like(surv_ref)
    own = [(base + t, t * SB_BLOCK) for t in reversed(range(SB_TILES))]
    scored = [score(j, r0, True) for j, r0 in own]
    for (j, r0), (log_w, total) in zip(own, scored):
        absorb(j, r0, True, log_w, total)

    def more(c):
        j, top = c
        return jnp.logical_and(j >= 0, top > SURV_FLOOR)

    def earlier_blocks(c):
        j, _ = c
        has_second = j >= 1
        j2 = jnp.maximum(j - 1, 0)
        log_w1, total1 = score(j, 0, False)
        log_w2, total2 = score(j2, 0, False)
        log_w2 = jnp.where(has_second, log_w2, -jnp.inf)
        total2 = jnp.where(has_second, total2, 0.0)
        absorb(j, 0, False, log_w1, total1)
        return j - 2, jnp.max(absorb(j2, 0, False, log_w2, total2))

    lax.while_loop(more, earlier_blocks, (base - 1, jnp.max(surv_ref[...])))


def _stick_breaking(qkv, batch, seq):
    n = qkv.shape[0]
    nh = SB_HEADS
    tq = SB_TILES * SB_BLOCK
    nq = seq // tq
    kk = lax.broadcasted_iota(jnp.int32, (SB_BLOCK, 2 * SB_BLOCK), 0)
    cc = lax.broadcasted_iota(jnp.int32, (SB_BLOCK, 2 * SB_BLOCK), 1)
    uo = jnp.where((kk > cc) | (cc >= SB_BLOCK), 1.0, 0.0).astype(BF16)
    uo = jnp.concatenate([uo, uo], axis=0)
    return pl.pallas_call(
        _sb_kernel,
        grid=(batch, nh, nq),
        in_specs=[
            pl.BlockSpec((tq, HEAD), lambda b, h, i: (b * nq + i, h)),
            pl.BlockSpec((seq, HEAD), lambda b, h, i: (b, nh + h)),
            pl.BlockSpec((seq, HEAD), lambda b, h, i: (b, 2 * nh + h)),
            pl.BlockSpec(uo.shape, lambda b, h, i: (0, 0)),
        ],
        out_specs=pl.BlockSpec((tq, HEAD), lambda b, h, i: (b * nq + i, h)),
        out_shape=jax.ShapeDtypeStruct((n, nh * HEAD), F32),
        scratch_shapes=[pltpu.VMEM((tq, HEAD), F32)],
        compiler_params=pltpu.CompilerParams(
            dimension_semantics=("parallel", "parallel", "arbitrary"), vmem_limit_bytes=VMEM_LIMIT),
        name="stick_breaking",
    )(qkv, qkv, qkv, uo)


def _hgrn_kernel(q_ref, f_ref, i_ref, g_ref, lbp_ref, onw_ref, sel_ref, o_ref, st_ref, *, layer):
    c = pl.program_id(2)

    @pl.when(c == 0)
    def _():
        st_ref[...] = jnp.zeros_like(st_ref)

    lbp = lbp_ref[...]
    e = jnp.exp(lbp - jnp.max(lbp, axis=0, keepdims=True))
    soft = e / jnp.sum(e, axis=0, keepdims=True)
    lb = soft[0:1, :]
    for l in range(1, layer + 1):
        lb = lb + soft[l:l + 1, :]
    lb = lb - soft[0:1, :]

    ff = f_ref[...]
    t = jnp.exp(-jnp.abs(ff))
    a = jnp.log(lb)
    b = jnp.log(1.0 - lb) + (jnp.minimum(ff, 0.0) - jnp.log(1.0 + t))
    log_f = jnp.maximum(a, b) + _log1p_exp(a - b)
    kk = (1.0 - lb) * (jnp.where(ff >= 0.0, t, 1.0) / (1.0 + t))
    qq = _silu(q_ref[...])
    vv = i_ref[...]
    gate = g_ref[...]
    onw = onw_ref[...]

    chunks = [slice(r, r + CHUNK) for r in range(0, q_ref.shape[0], CHUNK)]
    hi, lo = _split(log_f)
    runs = [jnp.dot(sel_ref[...], jnp.concatenate([hi[rs], lo[rs]], axis=0), preferred_element_type=F32)
            for rs in chunks]

    def block(b):
        return jnp.concatenate([r[b * CHUNK:(b + 1) * CHUNK] for r in runs], axis=0)

    q_dec = (qq * jnp.exp(block(0))).astype(BF16)
    k_dec = (kk * jnp.exp(block(1))).astype(BF16)
    g_lasts = [jnp.exp(r[CHUNK - 1:CHUNK, :]) for r in runs]

    row = lax.broadcasted_iota(jnp.int32, (CHUNK, CHUNK), 0)
    col = lax.broadcasted_iota(jnp.int32, (CHUNK, CHUNK), 1)
    rrow = lax.broadcasted_iota(jnp.int32, qq.shape, 0)
    heads = range(q_ref.shape[1] // HEAD)
    lanes = [slice(h * HEAD, (h + 1) * HEAD) for h in heads]
    pairs = [(rs, ln) for rs in chunks for ln in lanes]

    scores = [jnp.where(row == col, jnp.sum(qq[rs, ln] * kk[rs, ln], axis=1, keepdims=True), 0.0)
              for rs, ln in pairs]
    s, level = CHUNK, 2
    while s > 1:
        upper = (rrow & (s - 1)) >= s // 2
        same = (row & -s) == (col & -s)
        factor = jnp.exp(block(level))
        qt = jnp.where(upper, qq * factor, 0.0).astype(BF16)
        kt = jnp.where(upper, 0.0, kk * factor).astype(BF16)
        scores = [a + jnp.where(same, _mm_nt(qt[rs, ln], kt[rs, ln]), 0.0)
                  for a, (rs, ln) in zip(scores, pairs)]
        s, level = s // 2, level + 1

    states = [st_ref[h] for h in heads]
    for i, rs in enumerate(chunks):
        outs = [_mm_nt(q_dec[rs, ln], st) + _mm(scores[i * len(lanes) + h], vv[rs, ln])
                for h, (ln, st) in enumerate(zip(lanes, states))]
        states = [st * g_lasts[i][:, ln] + _mm_tn(vv[rs, ln], k_dec[rs, ln]) for ln, st in zip(lanes, states)]
        for h, ln in enumerate(lanes):
            o_ref[rs, ln] = _rms(outs[h], onw) * _silu(gate[rs, ln])
    for h in heads:
        st_ref[h] = states[h]


def _run_selectors():
    i = lax.broadcasted_iota(jnp.int32, (CHUNK, CHUNK), 0)
    r = lax.broadcasted_iota(jnp.int32, (CHUNK, CHUNK), 1)
    blocks = [r <= i, r > i]
    s = CHUNK
    while s > 1:
        mid = (i & -s) + s // 2
        blocks.append(jnp.where(i >= mid, (r > mid) & (r <= i), (r > i) & (r <= mid)))
        s //= 2
    sel = jnp.concatenate(blocks, axis=0).astype(BF16)
    return jnp.concatenate([sel, sel], axis=1)


def _hgrn(proj, lower_bounds, layer, out_norm, batch, seq):
    n = proj.shape[0]
    w = proj.shape[1] // 4
    gw = HGRN_GROUP * HEAD
    ng = w // gw
    tr = HGRN_CHUNKS * CHUNK
    nc = seq // tr
    sel = _run_selectors()
    rows = lambda b, c: b * nc + c
    return pl.pallas_call(
        functools.partial(_hgrn_kernel, layer=layer),
        grid=(batch, ng, nc),
        in_specs=[
            pl.BlockSpec((tr, gw), lambda b, g, c: (rows(b, c), g)),
            pl.BlockSpec((tr, gw), lambda b, g, c: (rows(b, c), ng + g)),
            pl.BlockSpec((tr, gw), lambda b, g, c: (rows(b, c), 2 * ng + g)),
            pl.BlockSpec((tr, gw), lambda b, g, c: (rows(b, c), 3 * ng + g)),
            pl.BlockSpec((lower_bounds.shape[0], gw), lambda b, g, c: (0, g)),
            pl.BlockSpec((1, HEAD), lambda b, g, c: (0, 0)),
            pl.BlockSpec(sel.shape, lambda b, g, c: (0, 0)),
        ],
        out_specs=pl.BlockSpec((tr, gw), lambda b, g, c: (rows(b, c), g)),
        out_shape=jax.ShapeDtypeStruct((n, w), F32),
        scratch_shapes=[pltpu.VMEM((HGRN_GROUP, HEAD, HEAD), F32)],
        compiler_params=pltpu.CompilerParams(
            dimension_semantics=("parallel", "parallel", "arbitrary"), vmem_limit_bytes=VMEM_LIMIT),
        name="hgrn2",
    )(proj, proj, proj, proj, lower_bounds.astype(F32), out_norm.reshape(1, HEAD), sel)


def _even_mixer(h, norm_w, w_in, conv_w, a_log, dt_bias, out_norm, w_out, batch, seq):
    gw, sw, nh = GDN_HEADS * HEAD, SB_HEADS * HEAD, GDN_HEADS
    o1 = 4 * gw
    o2 = o1 + 2 * nh
    w_gdn = jnp.concatenate([w_in[:, :o1], w_in[:, o1:o2],
                             jnp.zeros((w_in.shape[0], HEAD - 2 * nh), w_in.dtype)], axis=1)
    p_gdn = _norm_proj(h, norm_w, w_gdn, F32, tn=11 * HEAD)
    scale = jnp.concatenate([jnp.full((sw,), HEAD ** -0.5, F32), jnp.ones((2 * sw,), F32)])
    p_sb = _norm_proj(h, norm_w, w_in[:, o2:], BF16, tn=8 * HEAD, col_scale=scale)
    o_a = _gdn(p_gdn, conv_w, a_log, dt_bias, out_norm, batch, seq)
    o_b = _stick_breaking(p_sb, batch, seq)
    return _out_proj(h, [o_a, o_b], [w_out[:gw], w_out[gw:]])


def _odd_mixer(h, norm_w, w_in, lower_bounds, layer, out_norm, w_out, batch, seq):
    proj = _norm_proj(h, norm_w, w_in, F32, tn=8 * HEAD)
    o = _hgrn(proj, lower_bounds, layer, out_norm, batch, seq)
    return _out_proj(h, [o], [w_out])


def kernel(x, ffn1_norm, ffn1_w_gate, ffn1_w_up, ffn1_w_down, mix_norm, ffn2_norm, ffn2_w_gate, ffn2_w_up, ffn2_w_down, even_w_in, gdn_conv_w, gdn_a_log, gdn_dt_bias, gdn_out_norm, even_w_out, odd_w_in, hgrn_lower_bounds, hgrn_out_norm, odd_w_out, final_norm):
    batch, seq, d = x.shape
    depth = ffn1_norm.shape[0]
    h = x.reshape(batch * seq, d)
    for layer in range(depth):
        h = _ffn(h, ffn1_norm[layer], ffn1_w_gate[layer], ffn1_w_up[layer], ffn1_w_down[layer])
        m = layer // 2
        if layer % 2 == 0:
            h = _even_mixer(h, mix_norm[layer], even_w_in[m], gdn_conv_w[m], gdn_a_log[m], gdn_dt_bias[m],
                            gdn_out_norm[m], even_w_out[m], batch, seq)
        else:
            h = _odd_mixer(h, mix_norm[layer], odd_w_in[m], hgrn_lower_bounds, layer, hgrn_out_norm[m],
                           odd_w_out[m], batch, seq)
        h = _ffn(h, ffn2_norm[layer], ffn2_w_gate[layer], ffn2_w_up[layer], ffn2_w_down[layer],
                 final_w=final_norm if layer == depth - 1 else None)
    return h.reshape(batch, seq, d)
```

```python
import functools
import math

import jax
import jax.numpy as jnp
from jax import lax
from jax.experimental import pallas as pl
from jax.experimental.pallas import tpu as pltpu

F32 = jnp.float32
BF16 = jnp.bfloat16
EPS = 1e-6

HEAD = 128
GDN_HEADS = 8
SB_HEADS = 8
CHUNK = 64
GDN_CHUNKS = 2
SB_BLOCK = 128
SB_TILES = 4
SURV_FLOOR = -105.0
SUBLANES = 8

FFN_ROWS = 512
FFN_COLS = 512
FFN_VMEM_LIMIT = 48 * 1024 * 1024
PROJ_ROWS = 1024
OUT_ROWS = 256
HGRN_GROUP = 8
HGRN_CHUNKS = 2
VMEM_LIMIT = 48 * 1024 * 1024


def _mm(a, b):
    return jnp.dot(a.astype(BF16), b.astype(BF16), preferred_element_type=F32)


def _mm_nt(a, b):
    return lax.dot_general(a.astype(BF16), b.astype(BF16), (((1,), (1,)), ((), ())),
                           preferred_element_type=F32)


def _mm_tn(a, b):
    return lax.dot_general(a.astype(BF16), b.astype(BF16), (((0,), (0,)), ((), ())),
                           preferred_element_type=F32)


def _split(x):
    hi = x.astype(BF16)
    lo = (x - hi.astype(F32)).astype(BF16)
    return hi, lo


def _mm_hi(lhs, b, expand):
    m, n = lhs[0].shape[0], len(lhs)
    his, los = zip(*[_split(a) for a in lhs])
    bh, bl = (expand(t) for t in _split(b))
    top = jnp.dot(jnp.concatenate(his + los, axis=0), bh, preferred_element_type=F32)
    low = jnp.dot(jnp.concatenate(his, axis=0), bl, preferred_element_type=F32)
    rows = lambda t, i: t[i * m:(i + 1) * m]
    return [rows(top, i) + (rows(low, i) + rows(top, n + i)) for i in range(n)]


def _rms(x, w):
    return x * lax.rsqrt(jnp.mean(x * x, axis=-1, keepdims=True) + EPS) * w


def _silu(x):
    return x * jax.nn.sigmoid(x)


def _log1p_exp(x):
    return jnp.log(1.0 + jnp.exp(-jnp.abs(x)))


def _softplus(x):
    return jnp.maximum(x, 0.0) + _log1p_exp(x)


def _log_sigmoid(x):
    return jnp.minimum(x, 0.0) - _log1p_exp(x)


def _cumsum_rows(x):
    row = lax.broadcasted_iota(jnp.int32, x.shape, 0)
    s = 1
    while s < x.shape[0]:
        x = x + jnp.where(row >= s, pltpu.roll(x, s, 0), 0.0)
        s *= 2
    return x


def _shift_rows(x, prev, s):
    from_above = lax.broadcasted_iota(jnp.int32, prev.shape, 0) < s
    tiles = [prev] + [x[r:r + SUBLANES] for r in range(0, x.shape[0], SUBLANES)]
    turned = [pltpu.roll(t, s, 0) for t in tiles]
    return jnp.concatenate([jnp.where(from_above, above, here)
                            for above, here in zip(turned[:-1], turned[1:])], axis=0)


def _ffn_kernel(x_ref, nw_ref, wg_ref, wu_ref, wd_ref, fw_ref, o_ref, xn_ref, *, final_norm):
    j = pl.program_id(1)

    @pl.when(j == 0)
    def _():
        xn_ref[...] = _rms(x_ref[...], nw_ref[...]).astype(BF16)
        o_ref[...] = jnp.zeros_like(o_ref)

    xn = xn_ref[...]
    g = jnp.dot(xn, wg_ref[...], preferred_element_type=F32)
    u = jnp.dot(xn, wu_ref[...], preferred_element_type=F32)
    o_ref[...] += _mm(_silu(g) * u, wd_ref[...])

    @pl.when(j == pl.num_programs(1) - 1)
    def _():
        y = x_ref[...] + 0.5 * o_ref[...]
        if final_norm:
            y = _rms(y, fw_ref[...])
        o_ref[...] = y


def _ffn(h, norm_w, w_gate, w_up, w_down, final_w=None):
    n, d = h.shape
    f = w_gate.shape[1]
    tm, tf = min(FFN_ROWS, n), FFN_COLS
    fw = norm_w if final_w is None else final_w
    return pl.pallas_call(
        functools.partial(_ffn_kernel, final_norm=final_w is not None),
        grid=(n // tm, f // tf),
        in_specs=[
            pl.BlockSpec((tm, d), lambda i, j: (i, 0)),
            pl.BlockSpec((1, d), lambda i, j: (0, 0)),
            pl.BlockSpec((d, tf), lambda i, j: (0, j)),
            pl.BlockSpec((d, tf), lambda i, j: (0, j)),
            pl.BlockSpec((tf, d), lambda i, j: (j, 0)),
            pl.BlockSpec((1, d), lambda i, j: (0, 0)),
        ],
        out_specs=pl.BlockSpec((tm, d), lambda i, j: (i, 0)),
        out_shape=jax.ShapeDtypeStruct((n, d), F32),
        scratch_shapes=[pltpu.VMEM((tm, d), BF16)],
        compiler_params=pltpu.CompilerParams(
            dimension_semantics=("parallel", "arbitrary"), vmem_limit_bytes=FFN_VMEM_LIMIT),
        name="ffn",
    )(h, norm_w.reshape(1, d), w_gate.astype(BF16), w_up.astype(BF16), w_down.astype(BF16),
      fw.reshape(1, d))


def _norm_proj_kernel(x_ref, nw_ref, w_ref, cs_ref, o_ref, xn_ref, *, scaled):
    @pl.when(pl.program_id(1) == 0)
    def _():
        xn_ref[...] = _rms(x_ref[...], nw_ref[...]).astype(BF16)

    y = jnp.dot(xn_ref[...], w_ref[...], preferred_element_type=F32)
    if scaled:
        y = y * cs_ref[...]
    o_ref[...] = y.astype(o_ref.dtype)


def _norm_proj(h, norm_w, w, out_dtype, tn, col_scale=None):
    n, d = h.shape
    c = w.shape[1]
    tm = min(PROJ_ROWS, n)
    cs = jnp.ones((1, c), F32) if col_scale is None else col_scale.reshape(1, c)
    return pl.pallas_call(
        functools.partial(_norm_proj_kernel, scaled=col_scale is not None),
        grid=(n // tm, c // tn),
        in_specs=[
            pl.BlockSpec((tm, d), lambda i, j: (i, 0)),
            pl.BlockSpec((1, d), lambda i, j: (0, 0)),
            pl.BlockSpec((d, tn), lambda i, j: (0, j)),
            pl.BlockSpec((1, tn), lambda i, j: (0, j)),
        ],
        out_specs=pl.BlockSpec((tm, tn), lambda i, j: (i, j)),
        out_shape=jax.ShapeDtypeStruct((n, c), out_dtype),
        scratch_shapes=[pltpu.VMEM((tm, d), BF16)],
        compiler_params=pltpu.CompilerParams(
            dimension_semantics=("parallel", "arbitrary"), vmem_limit_bytes=VMEM_LIMIT),
        name="norm_proj",
    )(h, norm_w.reshape(1, d), w.astype(BF16), cs)


def _out_proj_kernel(*refs):
    h_ref, o_ref = refs[0], refs[-1]
    k = (len(refs) - 2) // 2
    acc = h_ref[...]
    for x_ref, w_ref in zip(refs[1:1 + k], refs[1 + k:1 + 2 * k]):
        acc = acc + _mm(x_ref[...], w_ref[...])
    o_ref[...] = acc


def _out_proj(h, xs, ws):
    n, d = h.shape
    tm = min(OUT_ROWS, n)
    return pl.pallas_call(
        _out_proj_kernel,
        grid=(n // tm,),
        in_specs=([pl.BlockSpec((tm, d), lambda i: (i, 0))]
                  + [pl.BlockSpec((tm, x.shape[1]), lambda i: (i, 0)) for x in xs]
                  + [pl.BlockSpec(w.shape, lambda i: (0, 0)) for w in ws]),
        out_specs=pl.BlockSpec((tm, d), lambda i: (i, 0)),
        out_shape=jax.ShapeDtypeStruct((n, d), F32),
        compiler_params=pltpu.CompilerParams(
            dimension_semantics=("parallel",), vmem_limit_bytes=VMEM_LIMIT),
        name="out_proj",
    )(h, *xs, *[w.astype(BF16) for w in ws])


def _unit_lower_inverses(lows, eye, expand):
    ps = [-low for low in lows]
    ts = [eye + p for p in ps]
    ps = [_mm_hi([p], p, expand)[0] for p in ps]
    s = 2
    while s < CHUNK:
        last = 2 * s >= CHUNK
        prods = [_mm_hi([t] if last else [t, p], p, expand) for t, p in zip(ts, ps)]
        ts = [t + pr[0] for t, pr in zip(ts, prods)]
        if not last:
            ps = [pr[1] for pr in prods]
        s *= 2
    return ts


def _gdn_kernel(q_ref, k_ref, v_ref, z_ref, ba_ref, cw_ref, alog_ref, dtb_ref, onw_ref,
                o_ref, s_ref, prev_ref):
    c = pl.program_id(1)
    nh, w = GDN_HEADS, GDN_HEADS * HEAD
    heads = range(nh)
    nrows = q_ref.shape[0]
    chunks = [slice(r, r + CHUNK) for r in range(0, nrows, CHUNK)]

    @pl.when(c == 0)
    def _():
        s_ref[...] = jnp.zeros_like(s_ref)
        prev_ref[...] = jnp.zeros_like(prev_ref)

    x = jnp.concatenate([q_ref[...], k_ref[...], v_ref[...]], axis=1)
    prev = prev_ref[...]
    cw = cw_ref[...]
    ntap = cw.shape[0]
    conv = None
    for j in range(ntap):
        s = ntap - 1 - j
        term = (x if s == 0 else _shift_rows(x, prev, s)) * cw[j:j + 1, :]
        conv = term if conv is None else conv + term
    prev_ref[...] = x[nrows - SUBLANES:, :]
    qkv = _silu(conv)

    ba = ba_ref[...]
    beta_t = jax.nn.sigmoid(ba)
    g_t = -jnp.exp(alog_ref[...]) * _softplus(ba + dtb_ref[...])
    gc_ts = [_cumsum_rows(g_t[rs]) for rs in chunks]
    gc_rows = [g.T for g in gc_ts]

    row = lax.broadcasted_iota(jnp.int32, (CHUNK, CHUNK), 0)
    col = lax.broadcasted_iota(jnp.int32, (CHUNK, CHUNK), 1)
    causal = row >= col
    strict = row > col
    z = z_ref[...]
    onw = onw_ref[...]

    def l2(t):
        return t * lax.rsqrt(jnp.sum(t * t, axis=-1, keepdims=True) + EPS)

    lanes = [slice(h * HEAD, (h + 1) * HEAD) for h in heads]
    qn = [l2(qkv[:, h * HEAD:(h + 1) * HEAD]) * (HEAD ** -0.5) for h in heads]
    kn = [l2(qkv[:, w + h * HEAD:w + (h + 1) * HEAD]) for h in heads]

    combos = [(i, h) for i in range(len(chunks)) for h in heads]
    duos = [(p, p + 1) for p in range(0, len(combos), 2)]
    qs = [qn[h][chunks[i]] for i, h in combos]
    ks = [kn[h][chunks[i]] for i, h in combos]
    vs = [qkv[chunks[i], 2 * w + h * HEAD:2 * w + (h + 1) * HEAD] for i, h in combos]
    betas = [beta_t[chunks[i], h:h + 1] for i, h in combos]
    gcols = [gc_ts[i][:, nh + h:nh + h + 1] for i, h in combos]
    grows = [gc_rows[i][nh + h:nh + h + 1, :] for i, h in combos]
    glasts = [g[CHUNK - 1:CHUNK, :] for g in gcols]
    kbs = [k * b for k, b in zip(ks, betas)]
    vbs = [v * b for v, b in zip(vs, betas)]
    kgs = [kb * jnp.exp(gc) for kb, gc in zip(kbs, gcols)]

    row2 = lax.broadcasted_iota(jnp.int32, (CHUNK, 2 * CHUNK), 0)
    lane2 = lax.broadcasted_iota(jnp.int32, (CHUNK, 2 * CHUNK), 1)
    left = lane2 < CHUNK
    col2 = lane2 & (CHUNK - 1)
    causal2 = row2 >= col2
    strict2 = row2 > col2
    eye2 = jnp.where(row2 == col2, 1.0, 0.0)
    zeros = jnp.zeros((CHUNK, HEAD), F32)

    def side_by_side(xa, xb):
        return jnp.concatenate([xa, xb], axis=1)

    def block_diag(t):
        return jnp.concatenate([jnp.where(left, t, 0), jnp.where(left, 0, t)], axis=0)

    decays2 = [jnp.where(causal2,
                         jnp.exp(jnp.where(causal2, jnp.where(left, gcols[a], gcols[b])
                                           - side_by_side(grows[a], grows[b]), 0.0)), 0.0)
               for a, b in duos]
    kq2 = [_mm_nt(jnp.concatenate([side_by_side(kbs[a], kbs[b]), side_by_side(qs[a], qs[b])], axis=0),
                  jnp.concatenate([side_by_side(ks[a], zeros), side_by_side(zeros, ks[b])], axis=0))
           for a, b in duos]
    lows2 = [jnp.where(strict2, kq[:CHUNK] * d, 0.0) for kq, d in zip(kq2, decays2)]
    a_qk2 = [(kq[CHUNK:] * d).astype(BF16) for kq, d in zip(kq2, decays2)]
    tinv2 = _unit_lower_inverses(lows2, eye2, block_diag)
    uw2 = [_mm(block_diag(t.astype(BF16)),
               jnp.concatenate([side_by_side(vbs[a], kgs[a]), side_by_side(vbs[b], kgs[b])], axis=0))
           for t, (a, b) in zip(tinv2, duos)]
    us = [uw[r:r + CHUNK, :HEAD] for uw in uw2 for r in (0, CHUNK)]
    ws = [uw[r:r + CHUNK, HEAD:] for uw in uw2 for r in (0, CHUNK)]
    q_decs = [q * jnp.exp(gc) for q, gc in zip(qs, gcols)]
    k_decs = [k * jnp.exp(gl - gc) for k, gl, gc in zip(ks, glasts, gcols)]
    wqs = [jnp.concatenate([wm, qd], axis=0) for wm, qd in zip(ws, q_decs)]

    states = [s_ref[h] for h in heads]
    for i, rs in enumerate(chunks):
        ids = [i * nh + h for h in heads]
        wq_ss = [_mm(wqs[p], st) for p, st in zip(ids, states)]
        v_news = [us[p] - wq[:CHUNK] for p, wq in zip(ids, wq_ss)]
        intra = [_mm(block_diag(a_qk2[p // 2]), jnp.concatenate([v_news[h], v_news[h + 1]], axis=0))
                 for p, h in zip(ids[::2], heads[::2])]
        outs = [wq[CHUNK:] + intra[h // 2][(h % 2) * CHUNK:(h % 2 + 1) * CHUNK] for h, wq in zip(heads, wq_ss)]
        states = [st * jnp.exp(glasts[p]) + _mm_tn(k_decs[p], vn) for p, st, vn in zip(ids, states, v_news)]
        for h in heads:
            o_ref[rs, lanes[h]] = _rms(outs[h], onw) * _silu(z[rs, lanes[h]])
    for h in heads:
        s_ref[h] = states[h]


def _gdn(proj, conv_w, a_log, dt_bias, out_norm, batch, seq):
    n = proj.shape[0]
    nh, w = GDN_HEADS, GDN_HEADS * HEAD
    rows = GDN_CHUNKS * CHUNK
    steps = seq // rows
    gate_block = 4 * w // HEAD
    pad = lambda p: jnp.zeros((1, HEAD), F32).at[0, nh:2 * nh].set(p.astype(F32))
    at = lambda b, c: b * steps + c
    return pl.pallas_call(
        _gdn_kernel,
        grid=(batch, steps),
        in_specs=[
            pl.BlockSpec((rows, w), lambda b, c: (at(b, c), 0)),
            pl.BlockSpec((rows, w), lambda b, c: (at(b, c), 1)),
            pl.BlockSpec((rows, w), lambda b, c: (at(b, c), 2)),
            pl.BlockSpec((rows, w), lambda b, c: (at(b, c), 3)),
            pl.BlockSpec((rows, HEAD), lambda b, c: (at(b, c), gate_block)),
            pl.BlockSpec(conv_w.shape, lambda b, c: (0, 0)),
            pl.BlockSpec((1, HEAD), lambda b, c: (0, 0)),
            pl.BlockSpec((1, HEAD), lambda b, c: (0, 0)),
            pl.BlockSpec((1, HEAD), lambda b, c: (0, 0)),
        ],
        out_specs=pl.BlockSpec((rows, w), lambda b, c: (at(b, c), 0)),
        out_shape=jax.ShapeDtypeStruct((n, w), F32),
        scratch_shapes=[pltpu.VMEM((nh, HEAD, HEAD), F32), pltpu.VMEM((SUBLANES, 3 * w), F32)],
        compiler_params=pltpu.CompilerParams(
            dimension_semantics=("parallel", "arbitrary"), vmem_limit_bytes=VMEM_LIMIT),
        name="gdn",
    )(proj, proj, proj, proj, proj, conv_w, pad(a_log), pad(dt_bias), out_norm.reshape(1, HEAD))


def _sb_kernel(q_ref, k_ref, v_ref, uo_ref, o_ref, surv_ref):
    base = pl.program_id(2) * SB_TILES
    uo = uo_ref[...]
    row = lax.broadcasted_iota(jnp.int32, (SB_BLOCK, SB_BLOCK), 0)
    col = lax.broadcasted_iota(jnp.int32, (SB_BLOCK, SB_BLOCK), 1)
    earlier = col < row

    def on_diagonal(t):
        top = jnp.where(earlier, t[:SB_BLOCK], 0.0)
        return top if t.shape[0] == SB_BLOCK else jnp.concatenate([top, t[SB_BLOCK:]], axis=0)

    def block(j):
        return pl.ds(pl.multiple_of(j * SB_BLOCK, SB_BLOCK), SB_BLOCK)

    def score(j, r0, diagonal):
        z = _mm_nt(q_ref[r0:, :], k_ref[block(j), :])
        log_beta = _log_sigmoid(z)
        log_fail = log_beta - z
        if diagonal:
            log_fail = on_diagonal(log_fail)
        hi, lo = _split(log_fail)
        sums = jnp.dot(jnp.concatenate([hi, lo], axis=1), uo, preferred_element_type=F32)
        return log_beta + sums[:, :SB_BLOCK], sums[:, SB_BLOCK:]

    def absorb(j, r0, diagonal, log_w, total):
        surv = surv_ref[r0:, :]
        wts = jnp.exp(log_w + surv)
        if diagonal:
            wts = on_diagonal(wts)
        o_ref[r0:, :] += _mm(wts, v_ref[block(j), :])
        surv = surv + total
        surv_ref[r0:, :] = surv
        return surv

    o_ref[...] = jnp.zeros_like(o_ref)
    surv_ref[...] = jnp.zeros_like(surv_ref)
    own = [(base + t, t * SB_BLOCK) for t in reversed(range(SB_TILES))]
    scored = [score(j, r0, True) for j, r0 in own]
    for (j, r0), (log_w, total) in zip(own, scored):
        absorb(j, r0, True, log_w, total)

    def more(c):
        j, top = c
        return jnp.logical_and(j >= 0, top > SURV_FLOOR)

    def earlier_blocks(c):
        j, _ = c
        has_second = j >= 1
        j2 = jnp.maximum(j - 1, 0)
        log_w1, total1 = score(j, 0, False)
        log_w2, total2 = score(j2, 0, False)
        log_w2 = jnp.where(has_second, log_w2, -jnp.inf)
        total2 = jnp.where(has_second, total2, 0.0)
        absorb(j, 0, False, log_w1, total1)
        return j - 2, jnp.max(absorb(j2, 0, False, log_w2, total2))

    lax.while_loop(more, earlier_blocks, (base - 1, jnp.max(surv_ref[...])))


def _stick_breaking(qkv, batch, seq):
    n = qkv.shape[0]
    nh = SB_HEADS
    tq = SB_TILES * SB_BLOCK
    nq = seq // tq
    kk = lax.broadcasted_iota(jnp.int32, (SB_BLOCK, 2 * SB_BLOCK), 0)
    cc = lax.broadcasted_iota(jnp.int32, (SB_BLOCK, 2 * SB_BLOCK), 1)
    uo = jnp.where((kk > cc) | (cc >= SB_BLOCK), 1.0, 0.0).astype(BF16)
    uo = jnp.concatenate([uo, uo], axis=0)
    return pl.pallas_call(
        _sb_kernel,
        grid=(batch, nh, nq),
        in_specs=[
            pl.BlockSpec((tq, HEAD), lambda b, h, i: (b * nq + i, h)),
            pl.BlockSpec((seq, HEAD), lambda b, h, i: (b, nh + h)),
            pl.BlockSpec((seq, HEAD), lambda b, h, i: (b, 2 * nh + h)),
            pl.BlockSpec(uo.shape, lambda b, h, i: (0, 0)),
        ],
        out_specs=pl.BlockSpec((tq, HEAD), lambda b, h, i: (b * nq + i, h)),
        out_shape=jax.ShapeDtypeStruct((n, nh * HEAD), F32),
        scratch_shapes=[pltpu.VMEM((tq, HEAD), F32)],
        compiler_params=pltpu.CompilerParams(
            dimension_semantics=("parallel", "parallel", "arbitrary"), vmem_limit_bytes=VMEM_LIMIT),
        name="stick_breaking",
    )(qkv, qkv, qkv, uo)


def _hgrn_kernel(q_ref, f_ref, i_ref, g_ref, lbp_ref, onw_ref, sel_ref, o_ref, st_ref, *, layer):
    c = pl.program_id(2)

    @pl.when(c == 0)
    def _():
        st_ref[...] = jnp.zeros_like(st_ref)

    lbp = lbp_ref[...]
    e = jnp.exp(lbp - jnp.max(lbp, axis=0, keepdims=True))
    soft = e / jnp.sum(e, axis=0, keepdims=True)
    lb = soft[0:1, :]
    for l in range(1, layer + 1):
        lb = lb + soft[l:l + 1, :]
    lb = lb - soft[0:1, :]

    ff = f_ref[...]
    t = jnp.exp(-jnp.abs(ff))
    a = jnp.log(lb)
    b = jnp.log(1.0 - lb) + (jnp.minimum(ff, 0.0) - jnp.log(1.0 + t))
    log_f = jnp.maximum(a, b) + _log1p_exp(a - b)
    kk = (1.0 - lb) * (jnp.where(ff >= 0.0, t, 1.0) / (1.0 + t))
    qq = _silu(q_ref[...])
    vv = i_ref[...]
    gate = g_ref[...]
    onw = onw_ref[...]

    chunks = [slice(r, r + CHUNK) for r in range(0, q_ref.shape[0], CHUNK)]
    hi, lo = _split(log_f)
    runs = [jnp.dot(sel_ref[...], jnp.concatenate([hi[rs], lo[rs]], axis=0), preferred_element_type=F32)
            for rs in chunks]

    def block(b):
        return jnp.concatenate([r[b * CHUNK:(b + 1) * CHUNK] for r in runs], axis=0)

    q_dec = (qq * jnp.exp(block(0))).astype(BF16)
    k_dec = (kk * jnp.exp(block(1))).astype(BF16)
    g_lasts = [jnp.exp(r[CHUNK - 1:CHUNK, :]) for r in runs]

    row = lax.broadcasted_iota(jnp.int32, (CHUNK, CHUNK), 0)
    col = lax.broadcasted_iota(jnp.int32, (CHUNK, CHUNK), 1)
    rrow = lax.broadcasted_iota(jnp.int32, qq.shape, 0)
    heads = range(q_ref.shape[1] // HEAD)
    lanes = [slice(h * HEAD, (h + 1) * HEAD) for h in heads]
    pairs = [(rs, ln) for rs in chunks for ln in lanes]

    scores = [jnp.where(row == col, jnp.sum(qq[rs, ln] * kk[rs, ln], axis=1, keepdims=True), 0.0)
              for rs, ln in pairs]
    s, level = CHUNK, 2
    while s > 1:
        upper = (rrow & (s - 1)) >= s // 2
        same = (row & -s) == (col & -s)
        factor = jnp.exp(block(level))
        qt = jnp.where(upper, qq * factor, 0.0).astype(BF16)
        kt = jnp.where(upper, 0.0, kk * factor).astype(BF16)
        scores = [a + jnp.where(same, _mm_nt(qt[rs, ln], kt[rs, ln]), 0.0)
                  for a, (rs, ln) in zip(scores, pairs)]
        s, level = s // 2, level + 1

    states = [st_ref[h] for h in heads]
    for i, rs in enumerate(chunks):
        outs = [_mm_nt(q_dec[rs, ln], st) + _mm(scores[i * len(lanes) + h], vv[rs, ln])
                for h, (ln, st) in enumerate(zip(lanes, states))]
        states = [st * g_lasts[i][:, ln] + _mm_tn(vv[rs, ln], k_dec[rs, ln]) for ln, st in zip(lanes, states)]
        for h, ln in enumerate(lanes):
            o_ref[rs, ln] = _rms(outs[h], onw) * _silu(gate[rs, ln])
    for h in heads:
        st_ref[h] = states[h]


def _run_selectors():
    i = lax.broadcasted_iota(jnp.int32, (CHUNK, CHUNK), 0)
    r = lax.broadcasted_iota(jnp.int32, (CHUNK, CHUNK), 1)
    blocks = [r <= i, r > i]
    s = CHUNK
    while s > 1:
        mid = (i & -s) + s // 2
        blocks.append(jnp.where(i >= mid, (r > mid) & (r <= i), (r > i) & (r <= mid)))
        s //= 2
    sel = jnp.concatenate(blocks, axis=0).astype(BF16)
    return jnp.concatenate([sel, sel], axis=1)


def _hgrn(proj, lower_bounds, layer, out_norm, batch, seq):
    n = proj.shape[0]
    w = proj.shape[1] // 4
    gw = HGRN_GROUP * HEAD
    ng = w // gw
    tr = HGRN_CHUNKS * CHUNK
    nc = seq // tr
    sel = _run_selectors()
    rows = lambda b, c: b * nc + c
    return pl.pallas_call(
        functools.partial(_hgrn_kernel, layer=layer),
        grid=(batch, ng, nc),
        in_specs=[
            pl.BlockSpec((tr, gw), lambda b, g, c: (rows(b, c), g)),
            pl.BlockSpec((tr, gw), lambda b, g, c: (rows(b, c), ng + g)),
            pl.BlockSpec((tr, gw), lambda b, g, c: (rows(b, c), 2 * ng + g)),
            pl.BlockSpec((tr, gw), lambda b, g, c: (rows(b, c), 3 * ng + g)),
            pl.BlockSpec((lower_bounds.shape[0], gw), lambda b, g, c: (0, g)),
            pl.BlockSpec((1, HEAD), lambda b, g, c: (0, 0)),
            pl.BlockSpec(sel.shape, lambda b, g, c: (0, 0)),
        ],
        out_specs=pl.BlockSpec((tr, gw), lambda b, g, c: (rows(b, c), g)),
        out_shape=jax.ShapeDtypeStruct((n, w), F32),
        scratch_shapes=[pltpu.VMEM((HGRN_GROUP, HEAD, HEAD), F32)],
        compiler_params=pltpu.CompilerParams(
            dimension_semantics=("parallel", "parallel", "arbitrary"), vmem_limit_bytes=VMEM_LIMIT),
        name="hgrn2",
    )(proj, proj, proj, proj, lower_bounds.astype(F32), out_norm.reshape(1, HEAD), sel)


def _even_mixer(h, norm_w, w_in, conv_w, a_log, dt_bias, out_norm, w_out, batch, seq):
    gw, sw, nh = GDN_HEADS * HEAD, SB_HEADS * HEAD, GDN_HEADS
    o1 = 4 * gw
    o2 = o1 + 2 * nh
    w_gdn = jnp.concatenate([w_in[:, :o1], w_in[:, o1:o2],
                             jnp.zeros((w_in.shape[0], HEAD - 2 * nh), w_in.dtype)], axis=1)
    p_gdn = _norm_proj(h, norm_w, w_gdn, F32, tn=11 * HEAD)
    scale = jnp.concatenate([jnp.full((sw,), HEAD ** -0.5, F32), jnp.ones((2 * sw,), F32)])
    p_sb = _norm_proj(h, norm_w, w_in[:, o2:], BF16, tn=8 * HEAD, col_scale=scale)
    o_a = _gdn(p_gdn, conv_w, a_log, dt_bias, out_norm, batch, seq)
    o_b = _stick_breaking(p_sb, batch, seq)
    return _out_proj(h, [o_a, o_b], [w_out[:gw], w_out[gw:]])


def _odd_mixer(h, norm_w, w_in, lower_bounds, layer, out_norm, w_out, batch, seq):
    proj = _norm_proj(h, norm_w, w_in, F32, tn=8 * HEAD)
    o = _hgrn(proj, lower_bounds, layer, out_norm, batch, seq)
    return _out_proj(h, [o], [w_out])


def kernel(x, ffn1_norm, ffn1_w_gate, ffn1_w_up, ffn1_w_down, mix_norm, ffn2_norm, ffn2_w_gate, ffn2_w_up, ffn2_w_down, even_w_in, gdn_conv_w, gdn_a_log, gdn_dt_bias, gdn_out_norm, even_w_out, odd_w_in, hgrn_lower_bounds, hgrn_out_norm, odd_w_out, final_norm):
    batch, seq, d = x.shape
    depth = ffn1_norm.shape[0]
    h = x.reshape(batch * seq, d)
    for layer in range(depth):
        h = _ffn(h, ffn1_norm[layer], ffn1_w_gate[layer], ffn1_w_up[layer], ffn1_w_down[layer])
        m = layer // 2
        if layer % 2 == 0:
            h = _even_mixer(h, mix_norm[layer], even_w_in[m], gdn_conv_w[m], gdn_a_log[m], gdn_dt_bias[m],
                            gdn_out_norm[m], even_w_out[m], batch, seq)
        else:
            h = _odd_mixer(h, mix_norm[layer], odd_w_in[m], hgrn_lower_bounds, layer, hgrn_out_norm[m],
                           odd_w_out[m], batch, seq)
        h = _ffn(h, ffn2_norm[layer], ffn2_w_gate[layer], ffn2_w_up[layer], ffn2_w_down[layer],
                 final_w=final_norm if layer == depth - 1 else None)
    return h.reshape(batch, seq, d)
```

```python
import functools
import math

import jax
import jax.numpy as jnp
from jax import lax
from jax.experimental import pallas as pl
from jax.experimental.pallas import tpu as pltpu

F32 = jnp.float32
BF16 = jnp.bfloat16
EPS = 1e-6

HEAD = 128
GDN_HEADS = 8
SB_HEADS = 8
CHUNK = 64
GDN_CHUNKS = 2
SB_BLOCK = 128
SB_TILES = 4
SURV_FLOOR = -105.0
SUBLANES = 8

FFN_ROWS = 1024
FFN_COLS = 512
FFN_SLAB = 256
FFN_VMEM_LIMIT = 60 * 1024 * 1024
PROJ_ROWS = 1024
OUT_ROWS = 256
HGRN_GROUP = 8
HGRN_CHUNKS = 2
VMEM_LIMIT = 48 * 1024 * 1024


def _mm(a, b):
    return jnp.dot(a.astype(BF16), b.astype(BF16), preferred_element_type=F32)


def _mm_nt(a, b):
    return lax.dot_general(a.astype(BF16), b.astype(BF16), (((1,), (1,)), ((), ())),
                           preferred_element_type=F32)


def _mm_tn(a, b):
    return lax.dot_general(a.astype(BF16), b.astype(BF16), (((0,), (0,)), ((), ())),
                           preferred_element_type=F32)


def _split(x):
    hi = x.astype(BF16)
    lo = (x - hi.astype(F32)).astype(BF16)
    return hi, lo


def _mm_hi(lhs, b, expand):
    m, n = lhs[0].shape[0], len(lhs)
    his, los = zip(*[_split(a) for a in lhs])
    bh, bl = (expand(t) for t in _split(b))
    top = jnp.dot(jnp.concatenate(his + los, axis=0), bh, preferred_element_type=F32)
    low = jnp.dot(jnp.concatenate(his, axis=0), bl, preferred_element_type=F32)
    rows = lambda t, i: t[i * m:(i + 1) * m]
    return [rows(top, i) + (rows(low, i) + rows(top, n + i)) for i in range(n)]


def _rms(x, w):
    return x * lax.rsqrt(jnp.mean(x * x, axis=-1, keepdims=True) + EPS) * w


def _silu(x):
    return x * jax.nn.sigmoid(x)


def _log1p_exp(x):
    return jnp.log(1.0 + jnp.exp(-jnp.abs(x)))


def _softplus(x):
    return jnp.maximum(x, 0.0) + _log1p_exp(x)


def _log_sigmoid(x):
    return jnp.minimum(x, 0.0) - _log1p_exp(x)


def _cumsum_rows(x):
    row = lax.broadcasted_iota(jnp.int32, x.shape, 0)
    s = 1
    while s < x.shape[0]:
        x = x + jnp.where(row >= s, pltpu.roll(x, s, 0), 0.0)
        s *= 2
    return x


def _shift_rows(x, prev, s):
    from_above = lax.broadcasted_iota(jnp.int32, prev.shape, 0) < s
    tiles = [prev] + [x[r:r + SUBLANES] for r in range(0, x.shape[0], SUBLANES)]
    turned = [pltpu.roll(t, s, 0) for t in tiles]
    return jnp.concatenate([jnp.where(from_above, above, here)
                            for above, here in zip(turned[:-1], turned[1:])], axis=0)


def _ffn_kernel(x_ref, nw_ref, wg_ref, wu_ref, wd_ref, fw_ref, o_ref, xn_ref, *, final_norm):
    j = pl.program_id(1)
    tm, d = x_ref.shape
    col_slabs = [slice(n, n + FFN_SLAB) for n in range(0, d, FFN_SLAB)]

    def for_row_slabs(body):
        def step(r, carry):
            body(pl.ds(pl.multiple_of(r * FFN_SLAB, FFN_SLAB), FFN_SLAB))
            return carry
        lax.fori_loop(0, tm // FFN_SLAB, step, 0)

    @pl.when(j == 0)
    def _():
        def normalise(rs):
            xn_ref[rs, :] = _rms(x_ref[rs, :], nw_ref[...]).astype(BF16)
        for_row_slabs(normalise)
        o_ref[...] = jnp.zeros_like(o_ref)

    xn = xn_ref[...]
    g = jnp.dot(xn, wg_ref[...], preferred_element_type=F32)
    u = jnp.dot(xn, wu_ref[...], preferred_element_type=F32)
    act = (_silu(g) * u).astype(BF16)
    for cs in col_slabs:
        o_ref[:, cs] += jnp.dot(act, wd_ref[:, cs], preferred_element_type=F32)

    @pl.when(j == pl.num_programs(1) - 1)
    def _():
        def finish(rs):
            y = x_ref[rs, :] + 0.5 * o_ref[rs, :]
            if final_norm:
                y = _rms(y, fw_ref[...])
            o_ref[rs, :] = y
        for_row_slabs(finish)


def _ffn(h, norm_w, w_gate, w_up, w_down, final_w=None):
    n, d = h.shape
    f = w_gate.shape[1]
    tm, tf = min(FFN_ROWS, n), FFN_COLS
    fw = norm_w if final_w is None else final_w
    return pl.pallas_call(
        functools.partial(_ffn_kernel, final_norm=final_w is not None),
        grid=(n // tm, f // tf),
        in_specs=[
            pl.BlockSpec((tm, d), lambda i, j: (i, 0)),
            pl.BlockSpec((1, d), lambda i, j: (0, 0)),
            pl.BlockSpec((d, tf), lambda i, j: (0, j)),
            pl.BlockSpec((d, tf), lambda i, j: (0, j)),
            pl.BlockSpec((tf, d), lambda i, j: (j, 0)),
            pl.BlockSpec((1, d), lambda i, j: (0, 0)),
        ],
        out_specs=pl.BlockSpec((tm, d), lambda i, j: (i, 0)),
        out_shape=jax.ShapeDtypeStruct((n, d), F32),
        scratch_shapes=[pltpu.VMEM((tm, d), BF16)],
        compiler_params=pltpu.CompilerParams(
            dimension_semantics=("parallel", "arbitrary"), vmem_limit_bytes=FFN_VMEM_LIMIT),
        name="ffn",
    )(h, norm_w.reshape(1, d), w_gate.astype(BF16), w_up.astype(BF16), w_down.astype(BF16),
      fw.reshape(1, d))


def _norm_proj_kernel(x_ref, nw_ref, w_ref, cs_ref, o_ref, xn_ref, *, scaled):
    @pl.when(pl.program_id(1) == 0)
    def _():
        xn_ref[...] = _rms(x_ref[...], nw_ref[...]).astype(BF16)

    y = jnp.dot(xn_ref[...], w_ref[...], preferred_element_type=F32)
    if scaled:
        y = y * cs_ref[...]
    o_ref[...] = y.astype(o_ref.dtype)


def _norm_proj(h, norm_w, w, out_dtype, tn, col_scale=None):
    n, d = h.shape
    c = w.shape[1]
    tm = min(PROJ_ROWS, n)
    cs = jnp.ones((1, c), F32) if col_scale is None else col_scale.reshape(1, c)
    return pl.pallas_call(
        functools.partial(_norm_proj_kernel, scaled=col_scale is not None),
        grid=(n // tm, c // tn),
        in_specs=[
            pl.BlockSpec((tm, d), lambda i, j: (i, 0)),
            pl.BlockSpec((1, d), lambda i, j: (0, 0)),
            pl.BlockSpec((d, tn), lambda i, j: (0, j)),
            pl.BlockSpec((1, tn), lambda i, j: (0, j)),
        ],
        out_specs=pl.BlockSpec((tm, tn), lambda i, j: (i, j)),
        out_shape=jax.ShapeDtypeStruct((n, c), out_dtype),
        scratch_shapes=[pltpu.VMEM((tm, d), BF16)],
        compiler_params=pltpu.CompilerParams(
            dimension_semantics=("parallel", "arbitrary"), vmem_limit_bytes=VMEM_LIMIT),
        name="norm_proj",
    )(h, norm_w.reshape(1, d), w.astype(BF16), cs)


def _out_proj_kernel(*refs):
    h_ref, o_ref = refs[0], refs[-1]
    k = (len(refs) - 2) // 2
    acc = h_ref[...]
    for x_ref, w_ref in zip(refs[1:1 + k], refs[1 + k:1 + 2 * k]):
        acc = acc + _mm(x_ref[...], w_ref[...])
    o_ref[...] = acc


def _out_proj(h, xs, ws):
    n, d = h.shape
    tm = min(OUT_ROWS, n)
    return pl.pallas_call(
        _out_proj_kernel,
        grid=(n // tm,),
        in_specs=([pl.BlockSpec((tm, d), lambda i: (i, 0))]
                  + [pl.BlockSpec((tm, x.shape[1]), lambda i: (i, 0)) for x in xs]
                  + [pl.BlockSpec(w.shape, lambda i: (0, 0)) for w in ws]),
        out_specs=pl.BlockSpec((tm, d), lambda i: (i, 0)),
        out_shape=jax.ShapeDtypeStruct((n, d), F32),
        compiler_params=pltpu.CompilerParams(
            dimension_semantics=("parallel",), vmem_limit_bytes=VMEM_LIMIT),
        name="out_proj",
    )(h, *xs, *[w.astype(BF16) for w in ws])


def _unit_lower_inverses(lows, eye, expand):
    ps = [-low for low in lows]
    ts = [eye + p for p in ps]
    ps = [_mm_hi([p], p, expand)[0] for p in ps]
    s = 2
    while s < CHUNK:
        last = 2 * s >= CHUNK
        prods = [_mm_hi([t] if last else [t, p], p, expand) for t, p in zip(ts, ps)]
        ts = [t + pr[0] for t, pr in zip(ts, prods)]
        if not last:
            ps = [pr[1] for pr in prods]
        s *= 2
    return ts


def _gdn_kernel(q_ref, k_ref, v_ref, z_ref, ba_ref, cw_ref, alog_ref, dtb_ref, onw_ref,
                o_ref, s_ref, prev_ref):
    c = pl.program_id(1)
    nh, w = GDN_HEADS, GDN_HEADS * HEAD
    heads = range(nh)
    nrows = q_ref.shape[0]
    chunks = [slice(r, r + CHUNK) for r in range(0, nrows, CHUNK)]

    @pl.when(c == 0)
    def _():
        s_ref[...] = jnp.zeros_like(s_ref)
        prev_ref[...] = jnp.zeros_like(prev_ref)

    x = jnp.concatenate([q_ref[...], k_ref[...], v_ref[...]], axis=1)
    prev = prev_ref[...]
    cw = cw_ref[...]
    ntap = cw.shape[0]
    conv = None
    for j in range(ntap):
        s = ntap - 1 - j
        term = (x if s == 0 else _shift_rows(x, prev, s)) * cw[j:j + 1, :]
        conv = term if conv is None else conv + term
    prev_ref[...] = x[nrows - SUBLANES:, :]
    qkv = _silu(conv)

    ba = ba_ref[...]
    beta_t = jax.nn.sigmoid(ba)
    g_t = -jnp.exp(alog_ref[...]) * _softplus(ba + dtb_ref[...])
    gc_ts = [_cumsum_rows(g_t[rs]) for rs in chunks]
    gc_rows = [g.T for g in gc_ts]

    row = lax.broadcasted_iota(jnp.int32, (CHUNK, CHUNK), 0)
    col = lax.broadcasted_iota(jnp.int32, (CHUNK, CHUNK), 1)
    causal = row >= col
    strict = row > col
    z = z_ref[...]
    onw = onw_ref[...]

    def l2(t):
        return t * lax.rsqrt(jnp.sum(t * t, axis=-1, keepdims=True) + EPS)

    lanes = [slice(h * HEAD, (h + 1) * HEAD) for h in heads]
    qn = [l2(qkv[:, h * HEAD:(h + 1) * HEAD]) * (HEAD ** -0.5) for h in heads]
    kn = [l2(qkv[:, w + h * HEAD:w + (h + 1) * HEAD]) for h in heads]

    combos = [(i, h) for i in range(len(chunks)) for h in heads]
    duos = [(p, p + 1) for p in range(0, len(combos), 2)]
    qs = [qn[h][chunks[i]] for i, h in combos]
    ks = [kn[h][chunks[i]] for i, h in combos]
    vs = [qkv[chunks[i], 2 * w + h * HEAD:2 * w + (h + 1) * HEAD] for i, h in combos]
    betas = [beta_t[chunks[i], h:h + 1] for i, h in combos]
    gcols = [gc_ts[i][:, nh + h:nh + h + 1] for i, h in combos]
    grows = [gc_rows[i][nh + h:nh + h + 1, :] for i, h in combos]
    glasts = [g[CHUNK - 1:CHUNK, :] for g in gcols]
    kbs = [k * b for k, b in zip(ks, betas)]
    vbs = [v * b for v, b in zip(vs, betas)]
    kgs = [kb * jnp.exp(gc) for kb, gc in zip(kbs, gcols)]

    row2 = lax.broadcasted_iota(jnp.int32, (CHUNK, 2 * CHUNK), 0)
    lane2 = lax.broadcasted_iota(jnp.int32, (CHUNK, 2 * CHUNK), 1)
    left = lane2 < CHUNK
    col2 = lane2 & (CHUNK - 1)
    causal2 = row2 >= col2
    strict2 = row2 > col2
    eye2 = jnp.where(row2 == col2, 1.0, 0.0)
    zeros = jnp.zeros((CHUNK, HEAD), F32)

    def side_by_side(xa, xb):
        return jnp.concatenate([xa, xb], axis=1)

    def block_diag(t):
        return jnp.concatenate([jnp.where(left, t, 0), jnp.where(left, 0, t)], axis=0)

    decays2 = [jnp.where(causal2,
                         jnp.exp(jnp.where(causal2, jnp.where(left, gcols[a], gcols[b])
                                           - side_by_side(grows[a], grows[b]), 0.0)), 0.0)
               for a, b in duos]
    kq2 = [_mm_nt(jnp.concatenate([side_by_side(kbs[a], kbs[b]), side_by_side(qs[a], qs[b])], axis=0),
                  jnp.concatenate([side_by_side(ks[a], zeros), side_by_side(zeros, ks[b])], axis=0))
           for a, b in duos]
    lows2 = [jnp.where(strict2, kq[:CHUNK] * d, 0.0) for kq, d in zip(kq2, decays2)]
    a_qk2 = [(kq[CHUNK:] * d).astype(BF16) for kq, d in zip(kq2, decays2)]
    tinv2 = _unit_lower_inverses(lows2, eye2, block_diag)
    uw2 = [_mm(block_diag(t.astype(BF16)),
               jnp.concatenate([side_by_side(vbs[a], kgs[a]), side_by_side(vbs[b], kgs[b])], axis=0))
           for t, (a, b) in zip(tinv2, duos)]
    us = [uw[r:r + CHUNK, :HEAD] for uw in uw2 for r in (0, CHUNK)]
    ws = [uw[r:r + CHUNK, HEAD:] for uw in uw2 for r in (0, CHUNK)]
    q_decs = [q * jnp.exp(gc) for q, gc in zip(qs, gcols)]
    k_decs = [k * jnp.exp(gl - gc) for k, gl, gc in zip(ks, glasts, gcols)]
    wqs = [jnp.concatenate([wm, qd], axis=0) for wm, qd in zip(ws, q_decs)]

    states = [s_ref[h] for h in heads]
    for i, rs in enumerate(chunks):
        ids = [i * nh + h for h in heads]
        wq_ss = [_mm(wqs[p], st) for p, st in zip(ids, states)]
        v_news = [us[p] - wq[:CHUNK] for p, wq in zip(ids, wq_ss)]
        intra = [_mm(block_diag(a_qk2[p // 2]), jnp.concatenate([v_news[h], v_news[h + 1]], axis=0))
                 for p, h in zip(ids[::2], heads[::2])]
        outs = [wq[CHUNK:] + intra[h // 2][(h % 2) * CHUNK:(h % 2 + 1) * CHUNK] for h, wq in zip(heads, wq_ss)]
        states = [st * jnp.exp(glasts[p]) + _mm_tn(k_decs[p], vn) for p, st, vn in zip(ids, states, v_news)]
        for h in heads:
            o_ref[rs, lanes[h]] = _rms(outs[h], onw) * _silu(z[rs, lanes[h]])
    for h in heads:
        s_ref[h] = states[h]


def _gdn(proj, conv_w, a_log, dt_bias, out_norm, batch, seq):
    n = proj.shape[0]
    nh, w = GDN_HEADS, GDN_HEADS * HEAD
    rows = GDN_CHUNKS * CHUNK
    steps = seq // rows
    gate_block = 4 * w // HEAD
    pad = lambda p: jnp.zeros((1, HEAD), F32).at[0, nh:2 * nh].set(p.astype(F32))
    at = lambda b, c: b * steps + c
    return pl.pallas_call(
        _gdn_kernel,
        grid=(batch, steps),
        in_specs=[
            pl.BlockSpec((rows, w), lambda b, c: (at(b, c), 0)),
            pl.BlockSpec((rows, w), lambda b, c: (at(b, c), 1)),
            pl.BlockSpec((rows, w), lambda b, c: (at(b, c), 2)),
            pl.BlockSpec((rows, w), lambda b, c: (at(b, c), 3)),
            pl.BlockSpec((rows, HEAD), lambda b, c: (at(b, c), gate_block)),
            pl.BlockSpec(conv_w.shape, lambda b, c: (0, 0)),
            pl.BlockSpec((1, HEAD), lambda b, c: (0, 0)),
            pl.BlockSpec((1, HEAD), lambda b, c: (0, 0)),
            pl.BlockSpec((1, HEAD), lambda b, c: (0, 0)),
        ],
        out_specs=pl.BlockSpec((rows, w), lambda b, c: (at(b, c), 0)),
        out_shape=jax.ShapeDtypeStruct((n, w), F32),
        scratch_shapes=[pltpu.VMEM((nh, HEAD, HEAD), F32), pltpu.VMEM((SUBLANES, 3 * w), F32)],
        compiler_params=pltpu.CompilerParams(
            dimension_semantics=("parallel", "arbitrary"), vmem_limit_bytes=VMEM_LIMIT),
        name="gdn",
    )(proj, proj, proj, proj, proj, conv_w, pad(a_log), pad(dt_bias), out_norm.reshape(1, HEAD))


def _sb_kernel(q_ref, k_ref, v_ref, uo_ref, o_ref, surv_ref):
    base = pl.program_id(2) * SB_TILES
    uo = uo_ref[...]
    row = lax.broadcasted_iota(jnp.int32, (SB_BLOCK, SB_BLOCK), 0)
    col = lax.broadcasted_iota(jnp.int32, (SB_BLOCK, SB_BLOCK), 1)
    earlier = col < row

    def on_diagonal(t):
        top = jnp.where(earlier, t[:SB_BLOCK], 0.0)
        return top if t.shape[0] == SB_BLOCK else jnp.concatenate([top, t[SB_BLOCK:]], axis=0)

    def block(j):
        return pl.ds(pl.multiple_of(j * SB_BLOCK, SB_BLOCK), SB_BLOCK)

    def score(j, r0, diagonal):
        z = _mm_nt(q_ref[r0:, :], k_ref[block(j), :])
        log_beta = _log_sigmoid(z)
        log_fail = log_beta - z
        if diagonal:
            log_fail = on_diagonal(log_fail)
        hi, lo = _split(log_fail)
        sums = jnp.dot(jnp.concatenate([hi, lo], axis=1), uo, preferred_element_type=F32)
        return log_beta + sums[:, :SB_BLOCK], sums[:, SB_BLOCK:]

    def absorb(j, r0, diagonal, log_w, total):
        surv = surv_ref[r0:, :]
        wts = jnp.exp(log_w + surv)
        if diagonal:
            wts = on_diagonal(wts)
        o_ref[r0:, :] += _mm(wts, v_ref[block(j), :])
        surv = surv + total
        surv_ref[r0:, :] = surv
        return surv

    o_ref[...] = jnp.zeros_like(o_ref)
    surv_ref[...] = jnp.zeros_like(surv_ref)
    own = [(base + t, t * SB_BLOCK) for t in reversed(range(SB_TILES))]
    scored = [score(j, r0, True) for j, r0 in own]
    for (j, r0), (log_w, total) in zip(own, scored):
        absorb(j, r0, True, log_w, total)

    def more(c):
        j, top = c
        return jnp.logical_and(j >= 0, top > SURV_FLOOR)

    def earlier_blocks(c):
        j, _ = c
        has_second = j >= 1
        j2 = jnp.maximum(j - 1, 0)
        log_w1, total1 = score(j, 0, False)
        log_w2, total2 = score(j2, 0, False)
        log_w2 = jnp.where(has_second, log_w2, -jnp.inf)
        total2 = jnp.where(has_second, total2, 0.0)
        absorb(j, 0, False, log_w1, total1)
        return j - 2, jnp.max(absorb(j2, 0, False, log_w2, total2))

    lax.while_loop(more, earlier_blocks, (base - 1, jnp.max(surv_ref[...])))


def _stick_breaking(qkv, batch, seq):
    n = qkv.shape[0]
    nh = SB_HEADS
    tq = SB_TILES * SB_BLOCK
    nq = seq // tq
    kk = lax.broadcasted_iota(jnp.int32, (SB_BLOCK, 2 * SB_BLOCK), 0)
    cc = lax.broadcasted_iota(jnp.int32, (SB_BLOCK, 2 * SB_BLOCK), 1)
    uo = jnp.where((kk > cc) | (cc >= SB_BLOCK), 1.0, 0.0).astype(BF16)
    uo = jnp.concatenate([uo, uo], axis=0)
    return pl.pallas_call(
        _sb_kernel,
        grid=(batch, nh, nq),
        in_specs=[
            pl.BlockSpec((tq, HEAD), lambda b, h, i: (b * nq + i, h)),
            pl.BlockSpec((seq, HEAD), lambda b, h, i: (b, nh + h)),
            pl.BlockSpec((seq, HEAD), lambda b, h, i: (b, 2 * nh + h)),
            pl.BlockSpec(uo.shape, lambda b, h, i: (0, 0)),
        ],
        out_specs=pl.BlockSpec((tq, HEAD), lambda b, h, i: (b * nq + i, h)),
        out_shape=jax.ShapeDtypeStruct((n, nh * HEAD), F32),
        scratch_shapes=[pltpu.VMEM((tq, HEAD), F32)],
        compiler_params=pltpu.CompilerParams(
            dimension_semantics=("parallel", "parallel", "arbitrary"), vmem_limit_bytes=VMEM_LIMIT),
        name="stick_breaking",
    )(qkv, qkv, qkv, uo)


def _hgrn_kernel(q_ref, f_ref, i_ref, g_ref, lbp_ref, onw_ref, sel_ref, o_ref, st_ref, *, layer):
    c = pl.program_id(2)

    @pl.when(c == 0)
    def _():
        st_ref[...] = jnp.zeros_like(st_ref)

    lbp = lbp_ref[...]
    e = jnp.exp(lbp - jnp.max(lbp, axis=0, keepdims=True))
    soft = e / jnp.sum(e, axis=0, keepdims=True)
    lb = soft[0:1, :]
    for l in range(1, layer + 1):
        lb = lb + soft[l:l + 1, :]
    lb = lb - soft[0:1, :]

    ff = f_ref[...]
    t = jnp.exp(-jnp.abs(ff))
    a = jnp.log(lb)
    b = jnp.log(1.0 - lb) + (jnp.minimum(ff, 0.0) - jnp.log(1.0 + t))
    log_f = jnp.maximum(a, b) + _log1p_exp(a - b)
    kk = (1.0 - lb) * (jnp.where(ff >= 0.0, t, 1.0) / (1.0 + t))
    qq = _silu(q_ref[...])
    vv = i_ref[...]
    gate = g_ref[...]
    onw = onw_ref[...]

    chunks = [slice(r, r + CHUNK) for r in range(0, q_ref.shape[0], CHUNK)]
    hi, lo = _split(log_f)
    runs = [jnp.dot(sel_ref[...], jnp.concatenate([hi[rs], lo[rs]], axis=0), preferred_element_type=F32)
            for rs in chunks]

    def block(b):
        return jnp.concatenate([r[b * CHUNK:(b + 1) * CHUNK] for r in runs], axis=0)

    q_dec = (qq * jnp.exp(block(0))).astype(BF16)
    k_dec = (kk * jnp.exp(block(1))).astype(BF16)
    g_lasts = [jnp.exp(r[CHUNK - 1:CHUNK, :]) for r in runs]

    rrow = lax.broadcasted_iota(jnp.int32, qq.shape, 0)
    heads = range(q_ref.shape[1] // HEAD)
    lanes = [slice(h * HEAD, (h + 1) * HEAD) for h in heads]
    lanes2 = [slice(h * HEAD, (h + 2) * HEAD) for h in heads[::2]]
    duos = [(rs, l2) for rs in chunks for l2 in lanes2]
    row2 = lax.broadcasted_iota(jnp.int32, (CHUNK, 2 * CHUNK), 0)
    lane2 = lax.broadcasted_iota(jnp.int32, (CHUNK, 2 * CHUNK), 1)
    left = lane2 < CHUNK
    col2 = lane2 & (CHUNK - 1)
    first = lax.broadcasted_iota(jnp.int32, (CHUNK, 2 * HEAD), 1) < HEAD

    def key_blocks(t):
        return jnp.concatenate([jnp.where(first, t, 0), jnp.where(first, 0, t)], axis=0)

    def score_blocks(t):
        return jnp.concatenate([jnp.where(left, t, 0), jnp.where(left, 0, t)], axis=0)

    qk = qq * kk
    scores = [jnp.where(row2 == col2,
                        jnp.where(left, jnp.sum(qk[rs, l2][:, :HEAD], axis=1, keepdims=True),
                                  jnp.sum(qk[rs, l2][:, HEAD:], axis=1, keepdims=True)), 0.0)
              for rs, l2 in duos]
    s, level = CHUNK, 2
    while s > 1:
        upper = (rrow & (s - 1)) >= s // 2
        same = (row2 & -s) == (col2 & -s)
        factor = jnp.exp(block(level))
        qt = jnp.where(upper, qq * factor, 0.0).astype(BF16)
        kt = jnp.where(upper, 0.0, kk * factor).astype(BF16)
        scores = [a + jnp.where(same, _mm_nt(qt[rs, l2], key_blocks(kt[rs, l2])), 0.0)
                  for a, (rs, l2) in zip(scores, duos)]
        s, level = s // 2, level + 1
    scores = [score_blocks(a.astype(BF16)) for a in scores]

    states = [st_ref[h] for h in heads]
    for i, rs in enumerate(chunks):
        intra = [_mm(scores[i * len(lanes2) + d], jnp.concatenate([vv[rs, lanes[2 * d]], vv[rs, lanes[2 * d + 1]]], axis=0))
                 for d in range(len(lanes2))]
        outs = [_mm_nt(q_dec[rs, ln], st) + intra[h // 2][(h % 2) * CHUNK:(h % 2 + 1) * CHUNK]
                for h, (ln, st) in enumerate(zip(lanes, states))]
        states = [st * g_lasts[i][:, ln] + _mm_tn(vv[rs, ln], k_dec[rs, ln]) for ln, st in zip(lanes, states)]
        for h, ln in enumerate(lanes):
            o_ref[rs, ln] = _rms(outs[h], onw) * _silu(gate[rs, ln])
    for h in heads:
        st_ref[h] = states[h]


def _run_selectors():
    i = lax.broadcasted_iota(jnp.int32, (CHUNK, CHUNK), 0)
    r = lax.broadcasted_iota(jnp.int32, (CHUNK, CHUNK), 1)
    blocks = [r <= i, r > i]
    s = CHUNK
    while s > 1:
        mid = (i & -s) + s // 2
        blocks.append(jnp.where(i >= mid, (r > mid) & (r <= i), (r > i) & (r <= mid)))
        s //= 2
    sel = jnp.concatenate(blocks, axis=0).astype(BF16)
    return jnp.concatenate([sel, sel], axis=1)


def _hgrn(proj, lower_bounds, layer, out_norm, batch, seq):
    n = proj.shape[0]
    w = proj.shape[1] // 4
    gw = HGRN_GROUP * HEAD
    ng = w // gw
    tr = HGRN_CHUNKS * CHUNK
    nc = seq // tr
    sel = _run_selectors()
    rows = lambda b, c: b * nc + c
    return pl.pallas_call(
        functools.partial(_hgrn_kernel, layer=layer),
        grid=(batch, ng, nc),
        in_specs=[
            pl.BlockSpec((tr, gw), lambda b, g, c: (rows(b, c), g)),
            pl.BlockSpec((tr, gw), lambda b, g, c: (rows(b, c), ng + g)),
            pl.BlockSpec((tr, gw), lambda b, g, c: (rows(b, c), 2 * ng + g)),
            pl.BlockSpec((tr, gw), lambda b, g, c: (rows(b, c), 3 * ng + g)),
            pl.BlockSpec((lower_bounds.shape[0], gw), lambda b, g, c: (0, g)),
            pl.BlockSpec((1, HEAD), lambda b, g, c: (0, 0)),
            pl.BlockSpec(sel.shape, lambda b, g, c: (0, 0)),
        ],
        out_specs=pl.BlockSpec((tr, gw), lambda b, g, c: (rows(b, c), g)),
        out_shape=jax.ShapeDtypeStruct((n, w), F32),
        scratch_shapes=[pltpu.VMEM((HGRN_GROUP, HEAD, HEAD), F32)],
        compiler_params=pltpu.CompilerParams(
            dimension_semantics=("parallel", "parallel", "arbitrary"), vmem_limit_bytes=VMEM_LIMIT),
        name="hgrn2",
    )(proj, proj, proj, proj, lower_bounds.astype(F32), out_norm.reshape(1, HEAD), sel)


def _even_mixer(h, norm_w, w_in, conv_w, a_log, dt_bias, out_norm, w_out, batch, seq):
    gw, sw, nh = GDN_HEADS * HEAD, SB_HEADS * HEAD, GDN_HEADS
    o1 = 4 * gw
    o2 = o1 + 2 * nh
    w_gdn = jnp.concatenate([w_in[:, :o1], w_in[:, o1:o2],
                             jnp.zeros((w_in.shape[0], HEAD - 2 * nh), w_in.dtype)], axis=1)
    p_gdn = _norm_proj(h, norm_w, w_gdn, F32, tn=11 * HEAD)
    scale = jnp.concatenate([jnp.full((sw,), HEAD ** -0.5, F32), jnp.ones((2 * sw,), F32)])
    p_sb = _norm_proj(h, norm_w, w_in[:, o2:], BF16, tn=8 * HEAD, col_scale=scale)
    o_a = _gdn(p_gdn, conv_w, a_log, dt_bias, out_norm, batch, seq)
    o_b = _stick_breaking(p_sb, batch, seq)
    return _out_proj(h, [o_a, o_b], [w_out[:gw], w_out[gw:]])


def _odd_mixer(h, norm_w, w_in, lower_bounds, layer, out_norm, w_out, batch, seq):
    proj = _norm_proj(h, norm_w, w_in, F32, tn=8 * HEAD)
    o = _hgrn(proj, lower_bounds, layer, out_norm, batch, seq)
    return _out_proj(h, [o], [w_out])


def kernel(x, ffn1_norm, ffn1_w_gate, ffn1_w_up, ffn1_w_down, mix_norm, ffn2_norm, ffn2_w_gate, ffn2_w_up, ffn2_w_down, even_w_in, gdn_conv_w, gdn_a_log, gdn_dt_bias, gdn_out_norm, even_w_out, odd_w_in, hgrn_lower_bounds, hgrn_out_norm, odd_w_out, final_norm):
    batch, seq, d = x.shape
    depth = ffn1_norm.shape[0]
    h = x.reshape(batch * seq, d)
    for layer in range(depth):
        h = _ffn(h, ffn1_norm[layer], ffn1_w_gate[layer], ffn1_w_up[layer], ffn1_w_down[layer])
        m = layer // 2
        if layer % 2 == 0:
            h = _even_mixer(h, mix_norm[layer], even_w_in[m], gdn_conv_w[m], gdn_a_log[m], gdn_dt_bias[m],
                            gdn_out_norm[m], even_w_out[m], batch, seq)
        else:
            h = _odd_mixer(h, mix_norm[layer], odd_w_in[m], hgrn_lower_bounds, layer, hgrn_out_norm[m],
                           odd_w_out[m], batch, seq)
        h = _ffn(h, ffn2_norm[layer], ffn2_w_gate[layer], ffn2_w_up[layer], ffn2_w_down[layer],
                 final_w=final_norm if layer == depth - 1 else None)
    return h.reshape(batch, seq, d)
```

```python
import functools
import math

import jax
import jax.numpy as jnp
from jax import lax
from jax.experimental import pallas as pl
from jax.experimental.pallas import tpu as pltpu

F32 = jnp.float32
BF16 = jnp.bfloat16
EPS = 1e-6

HEAD = 128
GDN_HEADS = 8
SB_HEADS = 8
CHUNK = 64
GDN_CHUNKS = 4
SB_BLOCK = 128
SB_TILES = 4
SURV_FLOOR = -105.0
SUBLANES = 8

FFN_ROWS = 1024
FFN_COLS = 512
FFN_SLAB = 256
FFN_VMEM_LIMIT = 60 * 1024 * 1024
PROJ_ROWS = 1024
OUT_ROWS = 512
HGRN_GROUP = 8
HGRN_CHUNKS = 4
VMEM_LIMIT = 48 * 1024 * 1024


def _mm(a, b):
    return jnp.dot(a.astype(BF16), b.astype(BF16), preferred_element_type=F32)


def _mm_nt(a, b):
    return lax.dot_general(a.astype(BF16), b.astype(BF16), (((1,), (1,)), ((), ())),
                           preferred_element_type=F32)


def _mm_tn(a, b):
    return lax.dot_general(a.astype(BF16), b.astype(BF16), (((0,), (0,)), ((), ())),
                           preferred_element_type=F32)


def _split(x):
    hi = x.astype(BF16)
    lo = (x - hi.astype(F32)).astype(BF16)
    return hi, lo


def _mm_hi(lhs, b, expand):
    m, n = lhs[0].shape[0], len(lhs)
    his, los = zip(*[_split(a) for a in lhs])
    bh, bl = (expand(t) for t in _split(b))
    top = jnp.dot(jnp.concatenate(his + los, axis=0), bh, preferred_element_type=F32)
    low = jnp.dot(jnp.concatenate(his, axis=0), bl, preferred_element_type=F32)
    rows = lambda t, i: t[i * m:(i + 1) * m]
    return [rows(top, i) + (rows(low, i) + rows(top, n + i)) for i in range(n)]


def _rms(x, w):
    return x * lax.rsqrt(jnp.mean(x * x, axis=-1, keepdims=True) + EPS) * w


def _silu(x):
    return x * jax.nn.sigmoid(x)


def _log1p_exp(x):
    return jnp.log(1.0 + jnp.exp(-jnp.abs(x)))


def _softplus(x):
    return jnp.maximum(x, 0.0) + _log1p_exp(x)


def _log_sigmoid(x):
    return jnp.minimum(x, 0.0) - _log1p_exp(x)


def _cumsum_rows(x):
    row = lax.broadcasted_iota(jnp.int32, x.shape, 0)
    s = 1
    while s < x.shape[0]:
        x = x + jnp.where(row >= s, pltpu.roll(x, s, 0), 0.0)
        s *= 2
    return x


def _shift_rows(x, prev, s):
    from_above = lax.broadcasted_iota(jnp.int32, prev.shape, 0) < s
    tiles = [prev] + [x[r:r + SUBLANES] for r in range(0, x.shape[0], SUBLANES)]
    turned = [pltpu.roll(t, s, 0) for t in tiles]
    return jnp.concatenate([jnp.where(from_above, above, here)
                            for above, here in zip(turned[:-1], turned[1:])], axis=0)


def _ffn_kernel(x_ref, nw_ref, wg_ref, wu_ref, wd_ref, fw_ref, o_ref, xn_ref, *, final_norm):
    j = pl.program_id(1)
    tm, d = x_ref.shape
    col_slabs = [slice(n, n + FFN_SLAB) for n in range(0, d, FFN_SLAB)]

    def for_row_slabs(body):
        def step(r, carry):
            body(pl.ds(pl.multiple_of(r * FFN_SLAB, FFN_SLAB), FFN_SLAB))
            return carry
        lax.fori_loop(0, tm // FFN_SLAB, step, 0)

    @pl.when(j == 0)
    def _():
        def normalise(rs):
            xn_ref[rs, :] = _rms(x_ref[rs, :], nw_ref[...]).astype(BF16)
        for_row_slabs(normalise)
        o_ref[...] = jnp.zeros_like(o_ref)

    xn = xn_ref[...]
    g = jnp.dot(xn, wg_ref[...], preferred_element_type=F32)
    u = jnp.dot(xn, wu_ref[...], preferred_element_type=F32)
    act = (_silu(g) * u).astype(BF16)
    for cs in col_slabs:
        o_ref[:, cs] += jnp.dot(act, wd_ref[:, cs], preferred_element_type=F32)

    @pl.when(j == pl.num_programs(1) - 1)
    def _():
        def finish(rs):
            y = x_ref[rs, :] + 0.5 * o_ref[rs, :]
            if final_norm:
                y = _rms(y, fw_ref[...])
            o_ref[rs, :] = y
        for_row_slabs(finish)


def _ffn(h, norm_w, w_gate, w_up, w_down, final_w=None):
    n, d = h.shape
    f = w_gate.shape[1]
    tm, tf = min(FFN_ROWS, n), FFN_COLS
    fw = norm_w if final_w is None else final_w
    return pl.pallas_call(
        functools.partial(_ffn_kernel, final_norm=final_w is not None),
        grid=(n // tm, f // tf),
        in_specs=[
            pl.BlockSpec((tm, d), lambda i, j: (i, 0)),
            pl.BlockSpec((1, d), lambda i, j: (0, 0)),
            pl.BlockSpec((d, tf), lambda i, j: (0, j)),
            pl.BlockSpec((d, tf), lambda i, j: (0, j)),
            pl.BlockSpec((tf, d), lambda i, j: (j, 0)),
            pl.BlockSpec((1, d), lambda i, j: (0, 0)),
        ],
        out_specs=pl.BlockSpec((tm, d), lambda i, j: (i, 0)),
        out_shape=jax.ShapeDtypeStruct((n, d), F32),
        scratch_shapes=[pltpu.VMEM((tm, d), BF16)],
        compiler_params=pltpu.CompilerParams(
            dimension_semantics=("parallel", "arbitrary"), vmem_limit_bytes=FFN_VMEM_LIMIT),
        name="ffn",
    )(h, norm_w.reshape(1, d), w_gate.astype(BF16), w_up.astype(BF16), w_down.astype(BF16),
      fw.reshape(1, d))


def _norm_proj_kernel(x_ref, nw_ref, w_ref, cs_ref, o_ref, xn_ref, *, scaled):
    @pl.when(pl.program_id(1) == 0)
    def _():
        xn_ref[...] = _rms(x_ref[...], nw_ref[...]).astype(BF16)

    y = jnp.dot(xn_ref[...], w_ref[...], preferred_element_type=F32)
    if scaled:
        y = y * cs_ref[...]
    o_ref[...] = y.astype(o_ref.dtype)


def _norm_proj(h, norm_w, w, out_dtype, tn, col_scale=None):
    n, d = h.shape
    c = w.shape[1]
    tm = min(PROJ_ROWS, n)
    cs = jnp.ones((1, c), F32) if col_scale is None else col_scale.reshape(1, c)
    return pl.pallas_call(
        functools.partial(_norm_proj_kernel, scaled=col_scale is not None),
        grid=(n // tm, c // tn),
        in_specs=[
            pl.BlockSpec((tm, d), lambda i, j: (i, 0)),
            pl.BlockSpec((1, d), lambda i, j: (0, 0)),
            pl.BlockSpec((d, tn), lambda i, j: (0, j)),
            pl.BlockSpec((1, tn), lambda i, j: (0, j)),
        ],
        out_specs=pl.BlockSpec((tm, tn), lambda i, j: (i, j)),
        out_shape=jax.ShapeDtypeStruct((n, c), out_dtype),
        scratch_shapes=[pltpu.VMEM((tm, d), BF16)],
        compiler_params=pltpu.CompilerParams(
            dimension_semantics=("parallel", "arbitrary"), vmem_limit_bytes=VMEM_LIMIT),
        name="norm_proj",
    )(h, norm_w.reshape(1, d), w.astype(BF16), cs)


def _out_proj_kernel(*refs):
    h_ref, o_ref = refs[0], refs[-1]
    k = (len(refs) - 2) // 2
    acc = h_ref[...]
    for x_ref, w_ref in zip(refs[1:1 + k], refs[1 + k:1 + 2 * k]):
        acc = acc + _mm(x_ref[...], w_ref[...])
    o_ref[...] = acc


def _out_proj(h, xs, ws):
    n, d = h.shape
    tm = min(OUT_ROWS, n)
    return pl.pallas_call(
        _out_proj_kernel,
        grid=(n // tm,),
        in_specs=([pl.BlockSpec((tm, d), lambda i: (i, 0))]
                  + [pl.BlockSpec((tm, x.shape[1]), lambda i: (i, 0)) for x in xs]
                  + [pl.BlockSpec(w.shape, lambda i: (0, 0)) for w in ws]),
        out_specs=pl.BlockSpec((tm, d), lambda i: (i, 0)),
        out_shape=jax.ShapeDtypeStruct((n, d), F32),
        compiler_params=pltpu.CompilerParams(
            dimension_semantics=("parallel",), vmem_limit_bytes=VMEM_LIMIT),
        name="out_proj",
    )(h, *xs, *[w.astype(BF16) for w in ws])


def _unit_lower_inverses(lows, eye, expand):
    ps = [-low for low in lows]
    ts = [eye + p for p in ps]
    ps = [_mm_hi([p], p, expand)[0] for p in ps]
    s = 2
    while s < CHUNK:
        last = 2 * s >= CHUNK
        prods = [_mm_hi([t] if last else [t, p], p, expand) for t, p in zip(ts, ps)]
        ts = [t + pr[0] for t, pr in zip(ts, prods)]
        if not last:
            ps = [pr[1] for pr in prods]
        s *= 2
    return ts


def _gdn_kernel(q_ref, k_ref, v_ref, z_ref, ba_ref, cw_ref, alog_ref, dtb_ref, onw_ref,
                o_ref, s_ref, prev_ref):
    c = pl.program_id(1)
    nh, w = GDN_HEADS, GDN_HEADS * HEAD
    heads = range(nh)
    nrows = q_ref.shape[0]
    chunks = [slice(r, r + CHUNK) for r in range(0, nrows, CHUNK)]

    @pl.when(c == 0)
    def _():
        s_ref[...] = jnp.zeros_like(s_ref)
        prev_ref[...] = jnp.zeros_like(prev_ref)

    x = jnp.concatenate([q_ref[...], k_ref[...], v_ref[...]], axis=1)
    prev = prev_ref[...]
    cw = cw_ref[...]
    ntap = cw.shape[0]
    conv = None
    for j in range(ntap):
        s = ntap - 1 - j
        term = (x if s == 0 else _shift_rows(x, prev, s)) * cw[j:j + 1, :]
        conv = term if conv is None else conv + term
    prev_ref[...] = x[nrows - SUBLANES:, :]
    qkv = _silu(conv)

    ba = ba_ref[...]
    beta_t = jax.nn.sigmoid(ba)
    g_t = -jnp.exp(alog_ref[...]) * _softplus(ba + dtb_ref[...])
    gc_ts = [_cumsum_rows(g_t[rs]) for rs in chunks]
    gc_rows = [g.T for g in gc_ts]

    row = lax.broadcasted_iota(jnp.int32, (CHUNK, CHUNK), 0)
    col = lax.broadcasted_iota(jnp.int32, (CHUNK, CHUNK), 1)
    causal = row >= col
    strict = row > col
    z = z_ref[...]
    onw = onw_ref[...]

    def l2(t):
        return t * lax.rsqrt(jnp.sum(t * t, axis=-1, keepdims=True) + EPS)

    lanes = [slice(h * HEAD, (h + 1) * HEAD) for h in heads]
    qn = [l2(qkv[:, h * HEAD:(h + 1) * HEAD]) * (HEAD ** -0.5) for h in heads]
    kn = [l2(qkv[:, w + h * HEAD:w + (h + 1) * HEAD]) for h in heads]

    combos = [(i, h) for i in range(len(chunks)) for h in heads]
    duos = [(p, p + 1) for p in range(0, len(combos), 2)]
    qs = [qn[h][chunks[i]] for i, h in combos]
    ks = [kn[h][chunks[i]] for i, h in combos]
    vs = [qkv[chunks[i], 2 * w + h * HEAD:2 * w + (h + 1) * HEAD] for i, h in combos]
    betas = [beta_t[chunks[i], h:h + 1] for i, h in combos]
    gcols = [gc_ts[i][:, nh + h:nh + h + 1] for i, h in combos]
    grows = [gc_rows[i][nh + h:nh + h + 1, :] for i, h in combos]
    glasts = [g[CHUNK - 1:CHUNK, :] for g in gcols]
    kbs = [k * b for k, b in zip(ks, betas)]
    vbs = [v * b for v, b in zip(vs, betas)]
    kgs = [kb * jnp.exp(gc) for kb, gc in zip(kbs, gcols)]

    row2 = lax.broadcasted_iota(jnp.int32, (CHUNK, 2 * CHUNK), 0)
    lane2 = lax.broadcasted_iota(jnp.int32, (CHUNK, 2 * CHUNK), 1)
    left = lane2 < CHUNK
    col2 = lane2 & (CHUNK - 1)
    causal2 = row2 >= col2
    strict2 = row2 > col2
    eye2 = jnp.where(row2 == col2, 1.0, 0.0)
    zeros = jnp.zeros((CHUNK, HEAD), F32)

    def side_by_side(xa, xb):
        return jnp.concatenate([xa, xb], axis=1)

    def block_diag(t):
        return jnp.concatenate([jnp.where(left, t, 0), jnp.where(left, 0, t)], axis=0)

    decays2 = [jnp.where(causal2,
                         jnp.exp(jnp.where(causal2, jnp.where(left, gcols[a], gcols[b])
                                           - side_by_side(grows[a], grows[b]), 0.0)), 0.0)
               for a, b in duos]
    kq2 = [_mm_nt(jnp.concatenate([side_by_side(kbs[a], kbs[b]), side_by_side(qs[a], qs[b])], axis=0),
                  jnp.concatenate([side_by_side(ks[a], zeros), side_by_side(zeros, ks[b])], axis=0))
           for a, b in duos]
    lows2 = [jnp.where(strict2, kq[:CHUNK] * d, 0.0) for kq, d in zip(kq2, decays2)]
    a_qk2 = [(kq[CHUNK:] * d).astype(BF16) for kq, d in zip(kq2, decays2)]
    tinv2 = _unit_lower_inverses(lows2, eye2, block_diag)
    uw2 = [_mm(block_diag(t.astype(BF16)),
               jnp.concatenate([side_by_side(vbs[a], kgs[a]), side_by_side(vbs[b], kgs[b])], axis=0))
           for t, (a, b) in zip(tinv2, duos)]
    us = [uw[r:r + CHUNK, :HEAD] for uw in uw2 for r in (0, CHUNK)]
    ws = [uw[r:r + CHUNK, HEAD:] for uw in uw2 for r in (0, CHUNK)]
    q_decs = [q * jnp.exp(gc) for q, gc in zip(qs, gcols)]
    k_decs = [k * jnp.exp(gl - gc) for k, gl, gc in zip(ks, glasts, gcols)]
    wqs = [jnp.concatenate([wm, qd], axis=0) for wm, qd in zip(ws, q_decs)]

    states = [s_ref[h] for h in heads]
    for i, rs in enumerate(chunks):
        ids = [i * nh + h for h in heads]
        wq_ss = [_mm(wqs[p], st) for p, st in zip(ids, states)]
        v_news = [us[p] - wq[:CHUNK] for p, wq in zip(ids, wq_ss)]
        intra = [_mm(block_diag(a_qk2[p // 2]), jnp.concatenate([v_news[h], v_news[h + 1]], axis=0))
                 for p, h in zip(ids[::2], heads[::2])]
        outs = [wq[CHUNK:] + intra[h // 2][(h % 2) * CHUNK:(h % 2 + 1) * CHUNK] for h, wq in zip(heads, wq_ss)]
        states = [st * jnp.exp(glasts[p]) + _mm_tn(k_decs[p], vn) for p, st, vn in zip(ids, states, v_news)]
        for h in heads:
            o_ref[rs, lanes[h]] = _rms(outs[h], onw) * _silu(z[rs, lanes[h]])
    for h in heads:
        s_ref[h] = states[h]


def _gdn(proj, conv_w, a_log, dt_bias, out_norm, batch, seq):
    n = proj.shape[0]
    nh, w = GDN_HEADS, GDN_HEADS * HEAD
    rows = GDN_CHUNKS * CHUNK
    steps = seq // rows
    gate_block = 4 * w // HEAD
    pad = lambda p: jnp.zeros((1, HEAD), F32).at[0, nh:2 * nh].set(p.astype(F32))
    at = lambda b, c: b * steps + c
    return pl.pallas_call(
        _gdn_kernel,
        grid=(batch, steps),
        in_specs=[
            pl.BlockSpec((rows, w), lambda b, c: (at(b, c), 0)),
            pl.BlockSpec((rows, w), lambda b, c: (at(b, c), 1)),
            pl.BlockSpec((rows, w), lambda b, c: (at(b, c), 2)),
            pl.BlockSpec((rows, w), lambda b, c: (at(b, c), 3)),
            pl.BlockSpec((rows, HEAD), lambda b, c: (at(b, c), gate_block)),
            pl.BlockSpec(conv_w.shape, lambda b, c: (0, 0)),
            pl.BlockSpec((1, HEAD), lambda b, c: (0, 0)),
            pl.BlockSpec((1, HEAD), lambda b, c: (0, 0)),
            pl.BlockSpec((1, HEAD), lambda b, c: (0, 0)),
        ],
        out_specs=pl.BlockSpec((rows, w), lambda b, c: (at(b, c), 0)),
        out_shape=jax.ShapeDtypeStruct((n, w), F32),
        scratch_shapes=[pltpu.VMEM((nh, HEAD, HEAD), F32), pltpu.VMEM((SUBLANES, 3 * w), F32)],
        compiler_params=pltpu.CompilerParams(
            dimension_semantics=("parallel", "arbitrary"), vmem_limit_bytes=VMEM_LIMIT),
        name="gdn",
    )(proj, proj, proj, proj, proj, conv_w, pad(a_log), pad(dt_bias), out_norm.reshape(1, HEAD))


def _sb_kernel(q_ref, k_ref, v_ref, uo_ref, o_ref, surv_ref):
    base = pl.program_id(2) * SB_TILES
    uo = uo_ref[...]
    row = lax.broadcasted_iota(jnp.int32, (SB_BLOCK, SB_BLOCK), 0)
    col = lax.broadcasted_iota(jnp.int32, (SB_BLOCK, SB_BLOCK), 1)
    earlier = col < row

    def on_diagonal(t):
        top = jnp.where(earlier, t[:SB_BLOCK], 0.0)
        return top if t.shape[0] == SB_BLOCK else jnp.concatenate([top, t[SB_BLOCK:]], axis=0)

    def block(j):
        return pl.ds(pl.multiple_of(j * SB_BLOCK, SB_BLOCK), SB_BLOCK)

    def score(j, r0, diagonal):
        z = _mm_nt(q_ref[r0:, :], k_ref[block(j), :])
        log_beta = _log_sigmoid(z)
        log_fail = log_beta - z
        if diagonal:
            log_fail = on_diagonal(log_fail)
        hi, lo = _split(log_fail)
        sums = jnp.dot(jnp.concatenate([hi, lo], axis=1), uo, preferred_element_type=F32)
        return log_beta + sums[:, :SB_BLOCK], sums[:, SB_BLOCK:]

    def absorb(j, r0, diagonal, log_w, total):
        surv = surv_ref[r0:, :]
        wts = jnp.exp(log_w + surv)
        if diagonal:
            wts = on_diagonal(wts)
        o_ref[r0:, :] += _mm(wts, v_ref[block(j), :])
        surv = surv + total
        surv_ref[r0:, :] = surv
        return surv

    o_ref[...] = jnp.zeros_like(o_ref)
    surv_ref[...] = jnp.zeros_like(surv_ref)
    own = [(base + t, t * SB_BLOCK) for t in reversed(range(SB_TILES))]
    scored = [score(j, r0, True) for j, r0 in own]

    def score_two_earlier(j):
        out = []
        for jj in (j, j - 1):
            log_w, total = score(jnp.maximum(jj, 0), 0, False)
            out.append((jnp.where(jj >= 0, log_w, -jnp.inf), jnp.where(jj >= 0, total, 0.0)))
        return out

    def absorb_two_earlier(j, two):
        for jj, (log_w, total) in zip((j, j - 1), two):
            surv = absorb(jnp.maximum(jj, 0), 0, False, log_w, total)
        return surv

    before = score_two_earlier(base - 1)
    for (j, r0), (log_w, total) in zip(own, scored):
        absorb(j, r0, True, log_w, total)
    surv = absorb_two_earlier(base - 1, before)

    def more(c):
        j, top = c
        return jnp.logical_and(j >= 0, top > SURV_FLOOR)

    def earlier_blocks(c):
        j, _ = c
        return j - 2, jnp.max(absorb_two_earlier(j, score_two_earlier(j)))

    lax.while_loop(more, earlier_blocks, (base - 3, jnp.max(surv)))


def _stick_breaking(qkv, batch, seq):
    n = qkv.shape[0]
    nh = SB_HEADS
    tq = SB_TILES * SB_BLOCK
    nq = seq // tq
    kk = lax.broadcasted_iota(jnp.int32, (SB_BLOCK, 2 * SB_BLOCK), 0)
    cc = lax.broadcasted_iota(jnp.int32, (SB_BLOCK, 2 * SB_BLOCK), 1)
    uo = jnp.where((kk > cc) | (cc >= SB_BLOCK), 1.0, 0.0).astype(BF16)
    uo = jnp.concatenate([uo, uo], axis=0)
    return pl.pallas_call(
        _sb_kernel,
        grid=(batch, nh, nq),
        in_specs=[
            pl.BlockSpec((tq, HEAD), lambda b, h, i: (b * nq + i, h)),
            pl.BlockSpec((seq, HEAD), lambda b, h, i: (b, nh + h)),
            pl.BlockSpec((seq, HEAD), lambda b, h, i: (b, 2 * nh + h)),
            pl.BlockSpec(uo.shape, lambda b, h, i: (0, 0)),
        ],
        out_specs=pl.BlockSpec((tq, HEAD), lambda b, h, i: (b * nq + i, h)),
        out_shape=jax.ShapeDtypeStruct((n, nh * HEAD), F32),
        scratch_shapes=[pltpu.VMEM((tq, HEAD), F32)],
        compiler_params=pltpu.CompilerParams(
            dimension_semantics=("parallel", "parallel", "arbitrary"), vmem_limit_bytes=VMEM_LIMIT),
        name="stick_breaking",
    )(qkv, qkv, qkv, uo)


def _hgrn_kernel(q_ref, f_ref, i_ref, g_ref, lbp_ref, onw_ref, sel_ref, o_ref, st_ref, *, layer):
    c = pl.program_id(2)

    @pl.when(c == 0)
    def _():
        st_ref[...] = jnp.zeros_like(st_ref)

    lbp = lbp_ref[...]
    e = jnp.exp(lbp - jnp.max(lbp, axis=0, keepdims=True))
    soft = e / jnp.sum(e, axis=0, keepdims=True)
    lb = soft[0:1, :]
    for l in range(1, layer + 1):
        lb = lb + soft[l:l + 1, :]
    lb = lb - soft[0:1, :]

    ff = f_ref[...]
    t = jnp.exp(-jnp.abs(ff))
    a = jnp.log(lb)
    b = jnp.log(1.0 - lb) + (jnp.minimum(ff, 0.0) - jnp.log(1.0 + t))
    log_f = jnp.maximum(a, b) + _log1p_exp(a - b)
    kk = (1.0 - lb) * (jnp.where(ff >= 0.0, t, 1.0) / (1.0 + t))
    qq = _silu(q_ref[...])
    vv = i_ref[...]
    gate = g_ref[...]
    onw = onw_ref[...]

    chunks = [slice(r, r + CHUNK) for r in range(0, q_ref.shape[0], CHUNK)]
    hi, lo = _split(log_f)
    runs = [jnp.dot(sel_ref[...], jnp.concatenate([hi[rs], lo[rs]], axis=0), preferred_element_type=F32)
            for rs in chunks]

    def block(b):
        return jnp.concatenate([r[b * CHUNK:(b + 1) * CHUNK] for r in runs], axis=0)

    q_dec = (qq * jnp.exp(block(0))).astype(BF16)
    k_dec = (kk * jnp.exp(block(1))).astype(BF16)
    g_lasts = [jnp.exp(r[CHUNK - 1:CHUNK, :]) for r in runs]

    rrow = lax.broadcasted_iota(jnp.int32, qq.shape, 0)
    heads = range(q_ref.shape[1] // HEAD)
    lanes = [slice(h * HEAD, (h + 1) * HEAD) for h in heads]
    lanes2 = [slice(h * HEAD, (h + 2) * HEAD) for h in heads[::2]]
    duos = [(rs, l2) for rs in chunks for l2 in lanes2]
    row2 = lax.broadcasted_iota(jnp.int32, (CHUNK, 2 * CHUNK), 0)
    lane2 = lax.broadcasted_iota(jnp.int32, (CHUNK, 2 * CHUNK), 1)
    left = lane2 < CHUNK
    col2 = lane2 & (CHUNK - 1)
    first = lax.broadcasted_iota(jnp.int32, (CHUNK, 2 * HEAD), 1) < HEAD

    def key_blocks(t):
        return jnp.concatenate([jnp.where(first, t, 0), jnp.where(first, 0, t)], axis=0)

    def score_blocks(t):
        return jnp.concatenate([jnp.where(left, t, 0), jnp.where(left, 0, t)], axis=0)

    qk = qq * kk
    scores = [jnp.where(row2 == col2,
                        jnp.where(left, jnp.sum(qk[rs, l2][:, :HEAD], axis=1, keepdims=True),
                                  jnp.sum(qk[rs, l2][:, HEAD:], axis=1, keepdims=True)), 0.0)
              for rs, l2 in duos]
    s, level = CHUNK, 2
    while s > 1:
        upper = (rrow & (s - 1)) >= s // 2
        same = (row2 & -s) == (col2 & -s)
        factor = jnp.exp(block(level))
        qt = jnp.where(upper, qq * factor, 0.0).astype(BF16)
        kt = jnp.where(upper, 0.0, kk * factor).astype(BF16)
        scores = [a + jnp.where(same, _mm_nt(qt[rs, l2], key_blocks(kt[rs, l2])), 0.0)
                  for a, (rs, l2) in zip(scores, duos)]
        s, level = s // 2, level + 1
    scores = [score_blocks(a.astype(BF16)) for a in scores]

    states = [st_ref[h] for h in heads]
    for i, rs in enumerate(chunks):
        intra = [_mm(scores[i * len(lanes2) + d], jnp.concatenate([vv[rs, lanes[2 * d]], vv[rs, lanes[2 * d + 1]]], axis=0))
                 for d in range(len(lanes2))]
        outs = [_mm_nt(q_dec[rs, ln], st) + intra[h // 2][(h % 2) * CHUNK:(h % 2 + 1) * CHUNK]
                for h, (ln, st) in enumerate(zip(lanes, states))]
        states = [st * g_lasts[i][:, ln] + _mm_tn(vv[rs, ln], k_dec[rs, ln]) for ln, st in zip(lanes, states)]
        for h, ln in enumerate(lanes):
            o_ref[rs, ln] = _rms(outs[h], onw) * _silu(gate[rs, ln])
    for h in heads:
        st_ref[h] = states[h]


def _run_selectors():
    i = lax.broadcasted_iota(jnp.int32, (CHUNK, CHUNK), 0)
    r = lax.broadcasted_iota(jnp.int32, (CHUNK, CHUNK), 1)
    blocks = [r <= i, r > i]
    s = CHUNK
    while s > 1:
        mid = (i & -s) + s // 2
        blocks.append(jnp.where(i >= mid, (r > mid) & (r <= i), (r > i) & (r <= mid)))
        s //= 2
    sel = jnp.concatenate(blocks, axis=0).astype(BF16)
    return jnp.concatenate([sel, sel], axis=1)


def _hgrn(proj, lower_bounds, layer, out_norm, batch, seq):
    n = proj.shape[0]
    w = proj.shape[1] // 4
    gw = HGRN_GROUP * HEAD
    ng = w // gw
    tr = HGRN_CHUNKS * CHUNK
    nc = seq // tr
    sel = _run_selectors()
    rows = lambda b, c: b * nc + c
    return pl.pallas_call(
        functools.partial(_hgrn_kernel, layer=layer),
        grid=(batch, ng, nc),
        in_specs=[
            pl.BlockSpec((tr, gw), lambda b, g, c: (rows(b, c), g)),
            pl.BlockSpec((tr, gw), lambda b, g, c: (rows(b, c), ng + g)),
            pl.BlockSpec((tr, gw), lambda b, g, c: (rows(b, c), 2 * ng + g)),
            pl.BlockSpec((tr, gw), lambda b, g, c: (rows(b, c), 3 * ng + g)),
            pl.BlockSpec((lower_bounds.shape[0], gw), lambda b, g, c: (0, g)),
            pl.BlockSpec((1, HEAD), lambda b, g, c: (0, 0)),
            pl.BlockSpec(sel.shape, lambda b, g, c: (0, 0)),
        ],
        out_specs=pl.BlockSpec((tr, gw), lambda b, g, c: (rows(b, c), g)),
        out_shape=jax.ShapeDtypeStruct((n, w), F32),
        scratch_shapes=[pltpu.VMEM((HGRN_GROUP, HEAD, HEAD), F32)],
        compiler_params=pltpu.CompilerParams(
            dimension_semantics=("parallel", "parallel", "arbitrary"), vmem_limit_bytes=VMEM_LIMIT),
        name="hgrn2",
    )(proj, proj, proj, proj, lower_bounds.astype(F32), out_norm.reshape(1, HEAD), sel)


def _even_mixer(h, norm_w, w_in, conv_w, a_log, dt_bias, out_norm, w_out, batch, seq):
    gw, sw, nh = GDN_HEADS * HEAD, SB_HEADS * HEAD, GDN_HEADS
    o1 = 4 * gw
    o2 = o1 + 2 * nh
    w_gdn = jnp.concatenate([w_in[:, :o1], w_in[:, o1:o2],
                             jnp.zeros((w_in.shape[0], HEAD - 2 * nh), w_in.dtype)], axis=1)
    p_gdn = _norm_proj(h, norm_w, w_gdn, F32, tn=11 * HEAD)
    scale = jnp.concatenate([jnp.full((sw,), HEAD ** -0.5, F32), jnp.ones((2 * sw,), F32)])
    p_sb = _norm_proj(h, norm_w, w_in[:, o2:], BF16, tn=8 * HEAD, col_scale=scale)
    o_a = _gdn(p_gdn, conv_w, a_log, dt_bias, out_norm, batch, seq)
    o_b = _stick_breaking(p_sb, batch, seq)
    return _out_proj(h, [o_a, o_b], [w_out[:gw], w_out[gw:]])


def _odd_mixer(h, norm_w, w_in, lower_bounds, layer, out_norm, w_out, batch, seq):
    proj = _norm_proj(h, norm_w, w_in, F32, tn=8 * HEAD)
    o = _hgrn(proj, lower_bounds, layer, out_norm, batch, seq)
    return _out_proj(h, [o], [w_out])


def kernel(x, ffn1_norm, ffn1_w_gate, ffn1_w_up, ffn1_w_down, mix_norm, ffn2_norm, ffn2_w_gate, ffn2_w_up, ffn2_w_down, even_w_in, gdn_conv_w, gdn_a_log, gdn_dt_bias, gdn_out_norm, even_w_out, odd_w_in, hgrn_lower_bounds, hgrn_out_norm, odd_w_out, final_norm):
    batch, seq, d = x.shape
    depth = ffn1_norm.shape[0]
    h = x.reshape(batch * seq, d)
    for layer in range(depth):
        h = _ffn(h, ffn1_norm[layer], ffn1_w_gate[layer], ffn1_w_up[layer], ffn1_w_down[layer])
        m = layer // 2
        if layer % 2 == 0:
            h = _even_mixer(h, mix_norm[layer], even_w_in[m], gdn_conv_w[m], gdn_a_log[m], gdn_dt_bias[m],
                            gdn_out_norm[m], even_w_out[m], batch, seq)
        else:
            h = _odd_mixer(h, mix_norm[layer], odd_w_in[m], hgrn_lower_bounds, layer, hgrn_out_norm[m],
                           odd_w_out[m], batch, seq)
        h = _ffn(h, ffn2_norm[layer], ffn2_w_gate[layer], ffn2_w_up[layer], ffn2_w_down[layer],
                 final_w=final_norm if layer == depth - 1 else None)
    return h.reshape(batch, seq, d)
```

```python
import functools
import math

import jax
import jax.numpy as jnp
from jax import lax
from jax.experimental import pallas as pl
from jax.experimental.pallas import tpu as pltpu

F32 = jnp.float32
BF16 = jnp.bfloat16
EPS = 1e-6

HEAD = 128
GDN_HEADS = 8
SB_HEADS = 8
CHUNK = 64
GDN_CHUNKS = 8
SB_BLOCK = 128
SB_TILES = 4
SURV_FLOOR = -105.0
SUBLANES = 8

FFN_ROWS = 1024
FFN_COLS = 512
FFN_SLAB = 256
FFN_VMEM_LIMIT = 60 * 1024 * 1024
PROJ_ROWS = 1024
OUT_ROWS = 512
HGRN_GROUP = 8
HGRN_CHUNKS = 8
VMEM_LIMIT = 48 * 1024 * 1024


def _mm(a, b):
    return jnp.dot(a.astype(BF16), b.astype(BF16), preferred_element_type=F32)


def _mm_nt(a, b):
    return lax.dot_general(a.astype(BF16), b.astype(BF16), (((1,), (1,)), ((), ())),
                           preferred_element_type=F32)


def _mm_tn(a, b):
    return lax.dot_general(a.astype(BF16), b.astype(BF16), (((0,), (0,)), ((), ())),
                           preferred_element_type=F32)


def _split(x):
    hi = x.astype(BF16)
    lo = (x - hi.astype(F32)).astype(BF16)
    return hi, lo


def _mm_hi(lhs, b, expand):
    m, n = lhs[0].shape[0], len(lhs)
    his, los = zip(*[_split(a) for a in lhs])
    bh, bl = (expand(t) for t in _split(b))
    top = jnp.dot(jnp.concatenate(his + los, axis=0), bh, preferred_element_type=F32)
    low = jnp.dot(jnp.concatenate(his, axis=0), bl, preferred_element_type=F32)
    rows = lambda t, i: t[i * m:(i + 1) * m]
    return [rows(top, i) + (rows(low, i) + rows(top, n + i)) for i in range(n)]


def _rms(x, w):
    return x * lax.rsqrt(jnp.mean(x * x, axis=-1, keepdims=True) + EPS) * w


def _silu(x):
    return x * jax.nn.sigmoid(x)


def _log1p_exp(x):
    return jnp.log(1.0 + jnp.exp(-jnp.abs(x)))


def _softplus(x):
    return jnp.maximum(x, 0.0) + _log1p_exp(x)


def _log_sigmoid(x):
    return jnp.minimum(x, 0.0) - _log1p_exp(x)


def _cumsum_rows(x):
    row = lax.broadcasted_iota(jnp.int32, x.shape, 0)
    s = 1
    while s < x.shape[0]:
        x = x + jnp.where(row >= s, pltpu.roll(x, s, 0), 0.0)
        s *= 2
    return x


def _shift_rows(x, prev, s):
    from_above = lax.broadcasted_iota(jnp.int32, prev.shape, 0) < s
    tiles = [prev] + [x[r:r + SUBLANES] for r in range(0, x.shape[0], SUBLANES)]
    turned = [pltpu.roll(t, s, 0) for t in tiles]
    return jnp.concatenate([jnp.where(from_above, above, here)
                            for above, here in zip(turned[:-1], turned[1:])], axis=0)


def _ffn_kernel(x_ref, nw_ref, wg_ref, wu_ref, wd_ref, fw_ref, o_ref, xn_ref, *, final_norm):
    j = pl.program_id(1)
    tm, d = x_ref.shape
    col_slabs = [slice(n, n + FFN_SLAB) for n in range(0, d, FFN_SLAB)]

    def for_row_slabs(body):
        def step(r, carry):
            body(pl.ds(pl.multiple_of(r * FFN_SLAB, FFN_SLAB), FFN_SLAB))
            return carry
        lax.fori_loop(0, tm // FFN_SLAB, step, 0)

    @pl.when(j == 0)
    def _():
        def normalise(rs):
            xn_ref[rs, :] = _rms(x_ref[rs, :], nw_ref[...]).astype(BF16)
        for_row_slabs(normalise)
        o_ref[...] = jnp.zeros_like(o_ref)

    xn = xn_ref[...]
    g = jnp.dot(xn, wg_ref[...], preferred_element_type=F32)
    u = jnp.dot(xn, wu_ref[...], preferred_element_type=F32)
    act = (_silu(g) * u).astype(BF16)
    for cs in col_slabs:
        o_ref[:, cs] += jnp.dot(act, wd_ref[:, cs], preferred_element_type=F32)

    @pl.when(j == pl.num_programs(1) - 1)
    def _():
        def finish(rs):
            y = x_ref[rs, :] + 0.5 * o_ref[rs, :]
            if final_norm:
                y = _rms(y, fw_ref[...])
            o_ref[rs, :] = y
        for_row_slabs(finish)


def _ffn(h, norm_w, w_gate, w_up, w_down, final_w=None):
    n, d = h.shape
    f = w_gate.shape[1]
    tm, tf = min(FFN_ROWS, n), FFN_COLS
    fw = norm_w if final_w is None else final_w
    return pl.pallas_call(
        functools.partial(_ffn_kernel, final_norm=final_w is not None),
        grid=(n // tm, f // tf),
        in_specs=[
            pl.BlockSpec((tm, d), lambda i, j: (i, 0)),
            pl.BlockSpec((1, d), lambda i, j: (0, 0)),
            pl.BlockSpec((d, tf), lambda i, j: (0, j)),
            pl.BlockSpec((d, tf), lambda i, j: (0, j)),
            pl.BlockSpec((tf, d), lambda i, j: (j, 0)),
            pl.BlockSpec((1, d), lambda i, j: (0, 0)),
        ],
        out_specs=pl.BlockSpec((tm, d), lambda i, j: (i, 0)),
        out_shape=jax.ShapeDtypeStruct((n, d), F32),
        scratch_shapes=[pltpu.VMEM((tm, d), BF16)],
        compiler_params=pltpu.CompilerParams(
            dimension_semantics=("parallel", "arbitrary"), vmem_limit_bytes=FFN_VMEM_LIMIT),
        name="ffn",
    )(h, norm_w.reshape(1, d), w_gate.astype(BF16), w_up.astype(BF16), w_down.astype(BF16),
      fw.reshape(1, d))


def _norm_proj_kernel(x_ref, nw_ref, w_ref, cs_ref, o_ref, xn_ref, *, scaled):
    @pl.when(pl.program_id(1) == 0)
    def _():
        xn_ref[...] = _rms(x_ref[...], nw_ref[...]).astype(BF16)

    y = jnp.dot(xn_ref[...], w_ref[...], preferred_element_type=F32)
    if scaled:
        y = y * cs_ref[...]
    o_ref[...] = y.astype(o_ref.dtype)


def _norm_proj(h, norm_w, w, out_dtype, tn, col_scale=None):
    n, d = h.shape
    c = w.shape[1]
    tm = min(PROJ_ROWS, n)
    cs = jnp.ones((1, c), F32) if col_scale is None else col_scale.reshape(1, c)
    return pl.pallas_call(
        functools.partial(_norm_proj_kernel, scaled=col_scale is not None),
        grid=(n // tm, c // tn),
        in_specs=[
            pl.BlockSpec((tm, d), lambda i, j: (i, 0)),
            pl.BlockSpec((1, d), lambda i, j: (0, 0)),
            pl.BlockSpec((d, tn), lambda i, j: (0, j)),
            pl.BlockSpec((1, tn), lambda i, j: (0, j)),
        ],
        out_specs=pl.BlockSpec((tm, tn), lambda i, j: (i, j)),
        out_shape=jax.ShapeDtypeStruct((n, c), out_dtype),
        scratch_shapes=[pltpu.VMEM((tm, d), BF16)],
        compiler_params=pltpu.CompilerParams(
            dimension_semantics=("parallel", "arbitrary"), vmem_limit_bytes=VMEM_LIMIT),
        name="norm_proj",
    )(h, norm_w.reshape(1, d), w.astype(BF16), cs)


def _out_proj_kernel(*refs):
    h_ref, o_ref = refs[0], refs[-1]
    k = (len(refs) - 2) // 2
    acc = h_ref[...]
    for x_ref, w_ref in zip(refs[1:1 + k], refs[1 + k:1 + 2 * k]):
        acc = acc + _mm(x_ref[...], w_ref[...])
    o_ref[...] = acc


def _out_proj(h, xs, ws):
    n, d = h.shape
    tm = min(OUT_ROWS, n)
    return pl.pallas_call(
        _out_proj_kernel,
        grid=(n // tm,),
        in_specs=([pl.BlockSpec((tm, d), lambda i: (i, 0))]
                  + [pl.BlockSpec((tm, x.shape[1]), lambda i: (i, 0)) for x in xs]
                  + [pl.BlockSpec(w.shape, lambda i: (0, 0)) for w in ws]),
        out_specs=pl.BlockSpec((tm, d), lambda i: (i, 0)),
        out_shape=jax.ShapeDtypeStruct((n, d), F32),
        compiler_params=pltpu.CompilerParams(
            dimension_semantics=("parallel",), vmem_limit_bytes=VMEM_LIMIT),
        name="out_proj",
    )(h, *xs, *[w.astype(BF16) for w in ws])


def _unit_lower_inverses(lows, eye, expand):
    ps = [-low for low in lows]
    ts = [eye + p for p in ps]
    ps = [_mm_hi([p], p, expand)[0] for p in ps]
    s = 2
    while s < CHUNK:
        last = 2 * s >= CHUNK
        prods = [_mm_hi([t] if last else [t, p], p, expand) for t, p in zip(ts, ps)]
        ts = [t + pr[0] for t, pr in zip(ts, prods)]
        if not last:
            ps = [pr[1] for pr in prods]
        s *= 2
    return ts


def _gdn_kernel(q_ref, k_ref, v_ref, z_ref, ba_ref, cw_ref, alog_ref, dtb_ref, onw_ref,
                o_ref, s_ref, prev_ref):
    c = pl.program_id(1)
    nh, w = GDN_HEADS, GDN_HEADS * HEAD
    heads = range(nh)
    nrows = q_ref.shape[0]
    chunks = [slice(r, r + CHUNK) for r in range(0, nrows, CHUNK)]

    @pl.when(c == 0)
    def _():
        s_ref[...] = jnp.zeros_like(s_ref)
        prev_ref[...] = jnp.zeros_like(prev_ref)

    x = jnp.concatenate([q_ref[...], k_ref[...], v_ref[...]], axis=1)
    prev = prev_ref[...]
    cw = cw_ref[...]
    ntap = cw.shape[0]
    conv = None
    for j in range(ntap):
        s = ntap - 1 - j
        term = (x if s == 0 else _shift_rows(x, prev, s)) * cw[j:j + 1, :]
        conv = term if conv is None else conv + term
    prev_ref[...] = x[nrows - SUBLANES:, :]
    qkv = _silu(conv)

    ba = ba_ref[...]
    beta_t = jax.nn.sigmoid(ba)
    g_t = -jnp.exp(alog_ref[...]) * _softplus(ba + dtb_ref[...])
    gc_ts = [_cumsum_rows(g_t[rs]) for rs in chunks]
    gc_rows = [g.T for g in gc_ts]

    row = lax.broadcasted_iota(jnp.int32, (CHUNK, CHUNK), 0)
    col = lax.broadcasted_iota(jnp.int32, (CHUNK, CHUNK), 1)
    causal = row >= col
    strict = row > col
    z = z_ref[...]
    onw = onw_ref[...]

    def l2(t):
        return t * lax.rsqrt(jnp.sum(t * t, axis=-1, keepdims=True) + EPS)

    lanes = [slice(h * HEAD, (h + 1) * HEAD) for h in heads]
    qn = [l2(qkv[:, h * HEAD:(h + 1) * HEAD]) * (HEAD ** -0.5) for h in heads]
    kn = [l2(qkv[:, w + h * HEAD:w + (h + 1) * HEAD]) for h in heads]

    combos = [(i, h) for i in range(len(chunks)) for h in heads]
    duos = [(p, p + 1) for p in range(0, len(combos), 2)]
    qs = [qn[h][chunks[i]] for i, h in combos]
    ks = [kn[h][chunks[i]] for i, h in combos]
    vs = [qkv[chunks[i], 2 * w + h * HEAD:2 * w + (h + 1) * HEAD] for i, h in combos]
    betas = [beta_t[chunks[i], h:h + 1] for i, h in combos]
    gcols = [gc_ts[i][:, nh + h:nh + h + 1] for i, h in combos]
    grows = [gc_rows[i][nh + h:nh + h + 1, :] for i, h in combos]
    glasts = [g[CHUNK - 1:CHUNK, :] for g in gcols]
    kbs = [k * b for k, b in zip(ks, betas)]
    vbs = [v * b for v, b in zip(vs, betas)]
    kgs = [kb * jnp.exp(gc) for kb, gc in zip(kbs, gcols)]

    row2 = lax.broadcasted_iota(jnp.int32, (CHUNK, 2 * CHUNK), 0)
    lane2 = lax.broadcasted_iota(jnp.int32, (CHUNK, 2 * CHUNK), 1)
    left = lane2 < CHUNK
    col2 = lane2 & (CHUNK - 1)
    causal2 = row2 >= col2
    strict2 = row2 > col2
    eye2 = jnp.where(row2 == col2, 1.0, 0.0)
    zeros = jnp.zeros((CHUNK, HEAD), F32)

    def side_by_side(xa, xb):
        return jnp.concatenate([xa, xb], axis=1)

    def block_diag(t):
        return jnp.concatenate([jnp.where(left, t, 0), jnp.where(left, 0, t)], axis=0)

    decays2 = [jnp.where(causal2,
                         jnp.exp(jnp.where(causal2, jnp.where(left, gcols[a], gcols[b])
                                           - side_by_side(grows[a], grows[b]), 0.0)), 0.0)
               for a, b in duos]
    kq2 = [_mm_nt(jnp.concatenate([side_by_side(kbs[a], kbs[b]), side_by_side(qs[a], qs[b])], axis=0),
                  jnp.concatenate([side_by_side(ks[a], zeros), side_by_side(zeros, ks[b])], axis=0))
           for a, b in duos]
    lows2 = [jnp.where(strict2, kq[:CHUNK] * d, 0.0) for kq, d in zip(kq2, decays2)]
    a_qk2 = [(kq[CHUNK:] * d).astype(BF16) for kq, d in zip(kq2, decays2)]
    tinv2 = _unit_lower_inverses(lows2, eye2, block_diag)
    uw2 = [_mm(block_diag(t.astype(BF16)),
               jnp.concatenate([side_by_side(vbs[a], kgs[a]), side_by_side(vbs[b], kgs[b])], axis=0))
           for t, (a, b) in zip(tinv2, duos)]
    us = [uw[r:r + CHUNK, :HEAD] for uw in uw2 for r in (0, CHUNK)]
    ws = [uw[r:r + CHUNK, HEAD:] for uw in uw2 for r in (0, CHUNK)]
    q_decs = [q * jnp.exp(gc) for q, gc in zip(qs, gcols)]
    k_decs = [k * jnp.exp(gl - gc) for k, gl, gc in zip(ks, glasts, gcols)]
    wqs = [jnp.concatenate([wm, qd], axis=0) for wm, qd in zip(ws, q_decs)]

    states = [s_ref[h] for h in heads]
    for i, rs in enumerate(chunks):
        ids = [i * nh + h for h in heads]
        wq_ss = [_mm(wqs[p], st) for p, st in zip(ids, states)]
        v_news = [us[p] - wq[:CHUNK] for p, wq in zip(ids, wq_ss)]
        intra = [_mm(block_diag(a_qk2[p // 2]), jnp.concatenate([v_news[h], v_news[h + 1]], axis=0))
                 for p, h in zip(ids[::2], heads[::2])]
        outs = [wq[CHUNK:] + intra[h // 2][(h % 2) * CHUNK:(h % 2 + 1) * CHUNK] for h, wq in zip(heads, wq_ss)]
        states = [st * jnp.exp(glasts[p]) + _mm_tn(k_decs[p], vn) for p, st, vn in zip(ids, states, v_news)]
        for h in heads:
            o_ref[rs, lanes[h]] = _rms(outs[h], onw) * _silu(z[rs, lanes[h]])
    for h in heads:
        s_ref[h] = states[h]


def _gdn(proj, conv_w, a_log, dt_bias, out_norm, batch, seq):
    n = proj.shape[0]
    nh, w = GDN_HEADS, GDN_HEADS * HEAD
    rows = GDN_CHUNKS * CHUNK
    steps = seq // rows
    gate_block = 4 * w // HEAD
    pad = lambda p: jnp.zeros((1, HEAD), F32).at[0, nh:2 * nh].set(p.astype(F32))
    at = lambda b, c: b * steps + c
    return pl.pallas_call(
        _gdn_kernel,
        grid=(batch, steps),
        in_specs=[
            pl.BlockSpec((rows, w), lambda b, c: (at(b, c), 0)),
            pl.BlockSpec((rows, w), lambda b, c: (at(b, c), 1)),
            pl.BlockSpec((rows, w), lambda b, c: (at(b, c), 2)),
            pl.BlockSpec((rows, w), lambda b, c: (at(b, c), 3)),
            pl.BlockSpec((rows, HEAD), lambda b, c: (at(b, c), gate_block)),
            pl.BlockSpec(conv_w.shape, lambda b, c: (0, 0)),
            pl.BlockSpec((1, HEAD), lambda b, c: (0, 0)),
            pl.BlockSpec((1, HEAD), lambda b, c: (0, 0)),
            pl.BlockSpec((1, HEAD), lambda b, c: (0, 0)),
        ],
        out_specs=pl.BlockSpec((rows, w), lambda b, c: (at(b, c), 0)),
        out_shape=jax.ShapeDtypeStruct((n, w), F32),
        scratch_shapes=[pltpu.VMEM((nh, HEAD, HEAD), F32), pltpu.VMEM((SUBLANES, 3 * w), F32)],
        compiler_params=pltpu.CompilerParams(
            dimension_semantics=("parallel", "arbitrary"), vmem_limit_bytes=VMEM_LIMIT),
        name="gdn",
    )(proj, proj, proj, proj, proj, conv_w, pad(a_log), pad(dt_bias), out_norm.reshape(1, HEAD))


def _sb_kernel(q_ref, k_ref, v_ref, uo_ref, o_ref, surv_ref):
    base = pl.program_id(2) * SB_TILES
    uo = uo_ref[...]
    row = lax.broadcasted_iota(jnp.int32, (SB_BLOCK, SB_BLOCK), 0)
    col = lax.broadcasted_iota(jnp.int32, (SB_BLOCK, SB_BLOCK), 1)
    earlier = col < row

    def on_diagonal(t):
        top = jnp.where(earlier, t[:SB_BLOCK], 0.0)
        return top if t.shape[0] == SB_BLOCK else jnp.concatenate([top, t[SB_BLOCK:]], axis=0)

    def block(j):
        return pl.ds(pl.multiple_of(j * SB_BLOCK, SB_BLOCK), SB_BLOCK)

    def score(j, r0, diagonal):
        z = _mm_nt(q_ref[r0:, :], k_ref[block(j), :])
        log_beta = _log_sigmoid(z)
        log_fail = log_beta - z
        if diagonal:
            log_fail = on_diagonal(log_fail)
        hi, lo = _split(log_fail)
        sums = jnp.dot(jnp.concatenate([hi, lo], axis=1), uo, preferred_element_type=F32)
        return log_beta + sums[:, :SB_BLOCK], sums[:, SB_BLOCK:]

    def absorb(j, r0, diagonal, log_w, total):
        surv = surv_ref[r0:, :]
        wts = jnp.exp(log_w + surv)
        if diagonal:
            wts = on_diagonal(wts)
        o_ref[r0:, :] += _mm(wts, v_ref[block(j), :])
        surv = surv + total
        surv_ref[r0:, :] = surv
        return surv

    o_ref[...] = jnp.zeros_like(o_ref)
    surv_ref[...] = jnp.zeros_like(surv_ref)
    own = [(base + t, t * SB_BLOCK) for t in reversed(range(SB_TILES))]
    scored = [score(j, r0, True) for j, r0 in own]

    def score_two_earlier(j):
        out = []
        for jj in (j, j - 1):
            log_w, total = score(jnp.maximum(jj, 0), 0, False)
            out.append((jnp.where(jj >= 0, log_w, -jnp.inf), jnp.where(jj >= 0, total, 0.0)))
        return out

    def absorb_two_earlier(j, two):
        for jj, (log_w, total) in zip((j, j - 1), two):
            surv = absorb(jnp.maximum(jj, 0), 0, False, log_w, total)
        return surv

    before = score_two_earlier(base - 1)
    for (j, r0), (log_w, total) in zip(own, scored):
        absorb(j, r0, True, log_w, total)
    surv = absorb_two_earlier(base - 1, before)

    def more(c):
        j, top = c
        return jnp.logical_and(j >= 0, top > SURV_FLOOR)

    def earlier_blocks(c):
        j, _ = c
        return j - 2, jnp.max(absorb_two_earlier(j, score_two_earlier(j)))

    lax.while_loop(more, earlier_blocks, (base - 3, jnp.max(surv)))


def _stick_breaking(qkv, batch, seq):
    n = qkv.shape[0]
    nh = SB_HEADS
    tq = SB_TILES * SB_BLOCK
    nq = seq // tq
    kk = lax.broadcasted_iota(jnp.int32, (SB_BLOCK, 2 * SB_BLOCK), 0)
    cc = lax.broadcasted_iota(jnp.int32, (SB_BLOCK, 2 * SB_BLOCK), 1)
    uo = jnp.where((kk > cc) | (cc >= SB_BLOCK), 1.0, 0.0).astype(BF16)
    uo = jnp.concatenate([uo, uo], axis=0)
    return pl.pallas_call(
        _sb_kernel,
        grid=(batch, nh, nq),
        in_specs=[
            pl.BlockSpec((tq, HEAD), lambda b, h, i: (b * nq + i, h)),
            pl.BlockSpec((seq, HEAD), lambda b, h, i: (b, nh + h)),
            pl.BlockSpec((seq, HEAD), lambda b, h, i: (b, 2 * nh + h)),
            pl.BlockSpec(uo.shape, lambda b, h, i: (0, 0)),
        ],
        out_specs=pl.BlockSpec((tq, HEAD), lambda b, h, i: (b * nq + i, h)),
        out_shape=jax.ShapeDtypeStruct((n, nh * HEAD), F32),
        scratch_shapes=[pltpu.VMEM((tq, HEAD), F32)],
        compiler_params=pltpu.CompilerParams(
            dimension_semantics=("parallel", "parallel", "arbitrary"), vmem_limit_bytes=VMEM_LIMIT),
        name="stick_breaking",
    )(qkv, qkv, qkv, uo)


def _hgrn_kernel(q_ref, f_ref, i_ref, g_ref, lbp_ref, onw_ref, sel_ref, o_ref, st_ref, *, layer):
    c = pl.program_id(2)

    @pl.when(c == 0)
    def _():
        st_ref[...] = jnp.zeros_like(st_ref)

    lbp = lbp_ref[...]
    e = jnp.exp(lbp - jnp.max(lbp, axis=0, keepdims=True))
    soft = e / jnp.sum(e, axis=0, keepdims=True)
    lb = soft[0:1, :]
    for l in range(1, layer + 1):
        lb = lb + soft[l:l + 1, :]
    lb = lb - soft[0:1, :]

    ff = f_ref[...]
    t = jnp.exp(-jnp.abs(ff))
    a = jnp.log(lb)
    b = jnp.log(1.0 - lb) + (jnp.minimum(ff, 0.0) - jnp.log(1.0 + t))
    log_f = jnp.maximum(a, b) + _log1p_exp(a - b)
    kk = (1.0 - lb) * (jnp.where(ff >= 0.0, t, 1.0) / (1.0 + t))
    qq = _silu(q_ref[...])
    vv = i_ref[...]
    gate = g_ref[...]
    onw = onw_ref[...]

    chunks = [slice(r, r + CHUNK) for r in range(0, q_ref.shape[0], CHUNK)]
    hi, lo = _split(log_f)
    runs = [jnp.dot(sel_ref[...], jnp.concatenate([hi[rs], lo[rs]], axis=0), preferred_element_type=F32)
            for rs in chunks]

    def block(b):
        return jnp.concatenate([r[b * CHUNK:(b + 1) * CHUNK] for r in runs], axis=0)

    q_dec = (qq * jnp.exp(block(0))).astype(BF16)
    k_dec = (kk * jnp.exp(block(1))).astype(BF16)
    g_lasts = [jnp.exp(r[CHUNK - 1:CHUNK, :]) for r in runs]

    rrow = lax.broadcasted_iota(jnp.int32, qq.shape, 0)
    heads = range(q_ref.shape[1] // HEAD)
    lanes = [slice(h * HEAD, (h + 1) * HEAD) for h in heads]
    lanes2 = [slice(h * HEAD, (h + 2) * HEAD) for h in heads[::2]]
    duos = [(rs, l2) for rs in chunks for l2 in lanes2]
    row2 = lax.broadcasted_iota(jnp.int32, (CHUNK, 2 * CHUNK), 0)
    lane2 = lax.broadcasted_iota(jnp.int32, (CHUNK, 2 * CHUNK), 1)
    left = lane2 < CHUNK
    col2 = lane2 & (CHUNK - 1)
    first = lax.broadcasted_iota(jnp.int32, (CHUNK, 2 * HEAD), 1) < HEAD

    def key_blocks(t):
        return jnp.concatenate([jnp.where(first, t, 0), jnp.where(first, 0, t)], axis=0)

    def score_blocks(t):
        return jnp.concatenate([jnp.where(left, t, 0), jnp.where(left, 0, t)], axis=0)

    qk = qq * kk
    scores = [jnp.where(row2 == col2,
                        jnp.where(left, jnp.sum(qk[rs, l2][:, :HEAD], axis=1, keepdims=True),
                                  jnp.sum(qk[rs, l2][:, HEAD:], axis=1, keepdims=True)), 0.0)
              for rs, l2 in duos]
    s, level = CHUNK, 2
    while s > 1:
        upper = (rrow & (s - 1)) >= s // 2
        same = (row2 & -s) == (col2 & -s)
        factor = jnp.exp(block(level))
        qt = jnp.where(upper, qq * factor, 0.0).astype(BF16)
        kt = jnp.where(upper, 0.0, kk * factor).astype(BF16)
        scores = [a + jnp.where(same, _mm_nt(qt[rs, l2], key_blocks(kt[rs, l2])), 0.0)
                  for a, (rs, l2) in zip(scores, duos)]
        s, level = s // 2, level + 1
    scores = [score_blocks(a.astype(BF16)) for a in scores]

    states = [st_ref[h] for h in heads]
    for i, rs in enumerate(chunks):
        intra = [_mm(scores[i * len(lanes2) + d], jnp.concatenate([vv[rs, lanes[2 * d]], vv[rs, lanes[2 * d + 1]]], axis=0))
                 for d in range(len(lanes2))]
        outs = [_mm_nt(q_dec[rs, ln], st) + intra[h // 2][(h % 2) * CHUNK:(h % 2 + 1) * CHUNK]
                for h, (ln, st) in enumerate(zip(lanes, states))]
        states = [st * g_lasts[i][:, ln] + _mm_tn(vv[rs, ln], k_dec[rs, ln]) for ln, st in zip(lanes, states)]
        for h, ln in enumerate(lanes):
            o_ref[rs, ln] = _rms(outs[h], onw) * _silu(gate[rs, ln])
    for h in heads:
        st_ref[h] = states[h]


def _run_selectors():
    i = lax.broadcasted_iota(jnp.int32, (CHUNK, CHUNK), 0)
    r = lax.broadcasted_iota(jnp.int32, (CHUNK, CHUNK), 1)
    blocks = [r <= i, r > i]
    s = CHUNK
    while s > 1:
        mid = (i & -s) + s // 2
        blocks.append(jnp.where(i >= mid, (r > mid) & (r <= i), (r > i) & (r <= mid)))
        s //= 2
    sel = jnp.concatenate(blocks, axis=0).astype(BF16)
    return jnp.concatenate([sel, sel], axis=1)


def _hgrn(proj, lower_bounds, layer, out_norm, batch, seq):
    n = proj.shape[0]
    w = proj.shape[1] // 4
    gw = HGRN_GROUP * HEAD
    ng = w // gw
    tr = HGRN_CHUNKS * CHUNK
    nc = seq // tr
    sel = _run_selectors()
    rows = lambda b, c: b * nc + c
    return pl.pallas_call(
        functools.partial(_hgrn_kernel, layer=layer),
        grid=(batch, ng, nc),
        in_specs=[
            pl.BlockSpec((tr, gw), lambda b, g, c: (rows(b, c), g)),
            pl.BlockSpec((tr, gw), lambda b, g, c: (rows(b, c), ng + g)),
            pl.BlockSpec((tr, gw), lambda b, g, c: (rows(b, c), 2 * ng + g)),
            pl.BlockSpec((tr, gw), lambda b, g, c: (rows(b, c), 3 * ng + g)),
            pl.BlockSpec((lower_bounds.shape[0], gw), lambda b, g, c: (0, g)),
            pl.BlockSpec((1, HEAD), lambda b, g, c: (0, 0)),
            pl.BlockSpec(sel.shape, lambda b, g, c: (0, 0)),
        ],
        out_specs=pl.BlockSpec((tr, gw), lambda b, g, c: (rows(b, c), g)),
        out_shape=jax.ShapeDtypeStruct((n, w), F32),
        scratch_shapes=[pltpu.VMEM((HGRN_GROUP, HEAD, HEAD), F32)],
        compiler_params=pltpu.CompilerParams(
            dimension_semantics=("parallel", "parallel", "arbitrary"), vmem_limit_bytes=VMEM_LIMIT),
        name="hgrn2",
    )(proj, proj, proj, proj, lower_bounds.astype(F32), out_norm.reshape(1, HEAD), sel)


def _even_mixer(h, norm_w, w_in, conv_w, a_log, dt_bias, out_norm, w_out, batch, seq):
    gw, sw, nh = GDN_HEADS * HEAD, SB_HEADS * HEAD, GDN_HEADS
    o1 = 4 * gw
    o2 = o1 + 2 * nh
    w_gdn = jnp.concatenate([w_in[:, :o1], w_in[:, o1:o2],
                             jnp.zeros((w_in.shape[0], HEAD - 2 * nh), w_in.dtype)], axis=1)
    p_gdn = _norm_proj(h, norm_w, w_gdn, F32, tn=11 * HEAD)
    scale = jnp.concatenate([jnp.full((sw,), HEAD ** -0.5, F32), jnp.ones((2 * sw,), F32)])
    p_sb = _norm_proj(h, norm_w, w_in[:, o2:], BF16, tn=8 * HEAD, col_scale=scale)
    o_a = _gdn(p_gdn, conv_w, a_log, dt_bias, out_norm, batch, seq)
    o_b = _stick_breaking(p_sb, batch, seq)
    return _out_proj(h, [o_a, o_b], [w_out[:gw], w_out[gw:]])


def _odd_mixer(h, norm_w, w_in, lower_bounds, layer, out_norm, w_out, batch, seq):
    proj = _norm_proj(h, norm_w, w_in, F32, tn=8 * HEAD)
    o = _hgrn(proj, lower_bounds, layer, out_norm, batch, seq)
    return _out_proj(h, [o], [w_out])


def kernel(x, ffn1_norm, ffn1_w_gate, ffn1_w_up, ffn1_w_down, mix_norm, ffn2_norm, ffn2_w_gate, ffn2_w_up, ffn2_w_down, even_w_in, gdn_conv_w, gdn_a_log, gdn_dt_bias, gdn_out_norm, even_w_out, odd_w_in, hgrn_lower_bounds, hgrn_out_norm, odd_w_out, final_norm):
    batch, seq, d = x.shape
    depth = ffn1_norm.shape[0]
    h = x.reshape(batch * seq, d)
    for layer in range(depth):
        h = _ffn(h, ffn1_norm[layer], ffn1_w_gate[layer], ffn1_w_up[layer], ffn1_w_down[layer])
        m = layer // 2
        if layer % 2 == 0:
            h = _even_mixer(h, mix_norm[layer], even_w_in[m], gdn_conv_w[m], gdn_a_log[m], gdn_dt_bias[m],
                            gdn_out_norm[m], even_w_out[m], batch, seq)
        else:
            h = _odd_mixer(h, mix_norm[layer], odd_w_in[m], hgrn_lower_bounds, layer, hgrn_out_norm[m],
                           odd_w_out[m], batch, seq)
        h = _ffn(h, ffn2_norm[layer], ffn2_w_gate[layer], ffn2_w_up[layer], ffn2_w_down[layer],
                 final_w=final_norm if layer == depth - 1 else None)
    return h.reshape(batch, seq, d)
```

```python
import functools
import math

import jax
import jax.numpy as jnp
from jax import lax
from jax.experimental import pallas as pl
from jax.experimental.pallas import tpu as pltpu

F32 = jnp.float32
BF16 = jnp.bfloat16
EPS = 1e-6

HEAD = 128
GDN_HEADS = 8
SB_HEADS = 8
CHUNK = 64
GDN_CHUNKS = 8
SB_BLOCK = 128
SB_TILES = 4
SURV_FLOOR = -105.0
SUBLANES = 8

FFN_ROWS = 1024
FFN_COLS = 512
FFN_SLAB = 256
FFN_VMEM_LIMIT = 60 * 1024 * 1024
PROJ_ROWS = 1024
OUT_ROWS = 512
HGRN_GROUP = 8
HGRN_CHUNKS = 8
VMEM_LIMIT = 48 * 1024 * 1024


def _mm(a, b):
    return jnp.dot(a.astype(BF16), b.astype(BF16), preferred_element_type=F32)


def _mm_nt(a, b):
    return lax.dot_general(a.astype(BF16), b.astype(BF16), (((1,), (1,)), ((), ())),
                           preferred_element_type=F32)


def _mm_tn(a, b):
    return lax.dot_general(a.astype(BF16), b.astype(BF16), (((0,), (0,)), ((), ())),
                           preferred_element_type=F32)


def _split(x):
    hi = x.astype(BF16)
    lo = (x - hi.astype(F32)).astype(BF16)
    return hi, lo


def _mm_hi(lhs, b, expand):
    m, n = lhs[0].shape[0], len(lhs)
    his, los = zip(*[_split(a) for a in lhs])
    bh, bl = (expand(t) for t in _split(b))
    top = jnp.dot(jnp.concatenate(his + los, axis=0), bh, preferred_element_type=F32)
    low = jnp.dot(jnp.concatenate(his, axis=0), bl, preferred_element_type=F32)
    rows = lambda t, i: t[i * m:(i + 1) * m]
    return [rows(top, i) + (rows(low, i) + rows(top, n + i)) for i in range(n)]


def _rms(x, w):
    return x * lax.rsqrt(jnp.mean(x * x, axis=-1, keepdims=True) + EPS) * w


def _silu(x):
    return x * jax.nn.sigmoid(x)


def _log1p_exp(x):
    return jnp.log(1.0 + jnp.exp(-jnp.abs(x)))


def _softplus(x):
    return jnp.maximum(x, 0.0) + _log1p_exp(x)


def _log_sigmoid(x):
    return jnp.minimum(x, 0.0) - _log1p_exp(x)


def _cumsum_rows(x):
    row = lax.broadcasted_iota(jnp.int32, x.shape, 0)
    s = 1
    while s < x.shape[0]:
        x = x + jnp.where(row >= s, pltpu.roll(x, s, 0), 0.0)
        s *= 2
    return x


def _shift_rows(x, prev, s):
    from_above = lax.broadcasted_iota(jnp.int32, prev.shape, 0) < s
    tiles = [prev] + [x[r:r + SUBLANES] for r in range(0, x.shape[0], SUBLANES)]
    turned = [pltpu.roll(t, s, 0) for t in tiles]
    return jnp.concatenate([jnp.where(from_above, above, here)
                            for above, here in zip(turned[:-1], turned[1:])], axis=0)


def _ffn_kernel(x_ref, nw_ref, wg_ref, wu_ref, wd_ref, fw_ref, o_ref, xn_ref, *, final_norm):
    j = pl.program_id(1)
    tm, d = x_ref.shape
    col_slabs = [slice(n, n + FFN_SLAB) for n in range(0, d, FFN_SLAB)]

    def for_row_slabs(body):
        def step(r, carry):
            body(pl.ds(pl.multiple_of(r * FFN_SLAB, FFN_SLAB), FFN_SLAB))
            return carry
        lax.fori_loop(0, tm // FFN_SLAB, step, 0)

    def hidden_step(first):
        xn = xn_ref[...]
        g = jnp.dot(xn, wg_ref[...], preferred_element_type=F32)
        u = jnp.dot(xn, wu_ref[...], preferred_element_type=F32)
        act = (_silu(g) * u).astype(BF16)
        for cs in col_slabs:
            part = jnp.dot(act, wd_ref[:, cs], preferred_element_type=F32)
            o_ref[:, cs] = part if first else o_ref[:, cs] + part

    @pl.when(j == 0)
    def _():
        def normalise(rs):
            xn_ref[rs, :] = _rms(x_ref[rs, :], nw_ref[...]).astype(BF16)
        for_row_slabs(normalise)
        hidden_step(first=True)

    @pl.when(j > 0)
    def _():
        hidden_step(first=False)

    @pl.when(j == pl.num_programs(1) - 1)
    def _():
        def finish(rs):
            y = x_ref[rs, :] + 0.5 * o_ref[rs, :]
            if final_norm:
                y = _rms(y, fw_ref[...])
            o_ref[rs, :] = y
        for_row_slabs(finish)


def _ffn(h, norm_w, w_gate, w_up, w_down, final_w=None):
    n, d = h.shape
    f = w_gate.shape[1]
    tm, tf = min(FFN_ROWS, n), FFN_COLS
    fw = norm_w if final_w is None else final_w
    return pl.pallas_call(
        functools.partial(_ffn_kernel, final_norm=final_w is not None),
        grid=(n // tm, f // tf),
        in_specs=[
            pl.BlockSpec((tm, d), lambda i, j: (i, 0)),
            pl.BlockSpec((1, d), lambda i, j: (0, 0)),
            pl.BlockSpec((d, tf), lambda i, j: (0, j)),
            pl.BlockSpec((d, tf), lambda i, j: (0, j)),
            pl.BlockSpec((tf, d), lambda i, j: (j, 0)),
            pl.BlockSpec((1, d), lambda i, j: (0, 0)),
        ],
        out_specs=pl.BlockSpec((tm, d), lambda i, j: (i, 0)),
        out_shape=jax.ShapeDtypeStruct((n, d), F32),
        scratch_shapes=[pltpu.VMEM((tm, d), BF16)],
        compiler_params=pltpu.CompilerParams(
            dimension_semantics=("parallel", "arbitrary"), vmem_limit_bytes=FFN_VMEM_LIMIT),
        name="ffn",
    )(h, norm_w.reshape(1, d), w_gate.astype(BF16), w_up.astype(BF16), w_down.astype(BF16),
      fw.reshape(1, d))


def _norm_proj_kernel(x_ref, nw_ref, w_ref, cs_ref, o_ref, xn_ref, *, scaled):
    @pl.when(pl.program_id(1) == 0)
    def _():
        xn_ref[...] = _rms(x_ref[...], nw_ref[...]).astype(BF16)

    y = jnp.dot(xn_ref[...], w_ref[...], preferred_element_type=F32)
    if scaled:
        y = y * cs_ref[...]
    o_ref[...] = y.astype(o_ref.dtype)


def _norm_proj(h, norm_w, w, out_dtype, tn, col_scale=None):
    n, d = h.shape
    c = w.shape[1]
    tm = min(PROJ_ROWS, n)
    cs = jnp.ones((1, c), F32) if col_scale is None else col_scale.reshape(1, c)
    return pl.pallas_call(
        functools.partial(_norm_proj_kernel, scaled=col_scale is not None),
        grid=(n // tm, c // tn),
        in_specs=[
            pl.BlockSpec((tm, d), lambda i, j: (i, 0)),
            pl.BlockSpec((1, d), lambda i, j: (0, 0)),
            pl.BlockSpec((d, tn), lambda i, j: (0, j)),
            pl.BlockSpec((1, tn), lambda i, j: (0, j)),
        ],
        out_specs=pl.BlockSpec((tm, tn), lambda i, j: (i, j)),
        out_shape=jax.ShapeDtypeStruct((n, c), out_dtype),
        scratch_shapes=[pltpu.VMEM((tm, d), BF16)],
        compiler_params=pltpu.CompilerParams(
            dimension_semantics=("parallel", "arbitrary"), vmem_limit_bytes=VMEM_LIMIT),
        name="norm_proj",
    )(h, norm_w.reshape(1, d), w.astype(BF16), cs)


def _out_proj_kernel(*refs):
    h_ref, o_ref = refs[0], refs[-1]
    k = (len(refs) - 2) // 2
    acc = h_ref[...]
    for x_ref, w_ref in zip(refs[1:1 + k], refs[1 + k:1 + 2 * k]):
        acc = acc + _mm(x_ref[...], w_ref[...])
    o_ref[...] = acc


def _out_proj(h, xs, ws):
    n, d = h.shape
    tm = min(OUT_ROWS, n)
    return pl.pallas_call(
        _out_proj_kernel,
        grid=(n // tm,),
        in_specs=([pl.BlockSpec((tm, d), lambda i: (i, 0))]
                  + [pl.BlockSpec((tm, x.shape[1]), lambda i: (i, 0)) for x in xs]
                  + [pl.BlockSpec(w.shape, lambda i: (0, 0)) for w in ws]),
        out_specs=pl.BlockSpec((tm, d), lambda i: (i, 0)),
        out_shape=jax.ShapeDtypeStruct((n, d), F32),
        compiler_params=pltpu.CompilerParams(
            dimension_semantics=("parallel",), vmem_limit_bytes=VMEM_LIMIT),
        name="out_proj",
    )(h, *xs, *[w.astype(BF16) for w in ws])


def _unit_lower_inverses(lows, eye, expand):
    ps = [-low for low in lows]
    ts = [eye + p for p in ps]
    ps = [_mm_hi([p], p, expand)[0] for p in ps]
    s = 2
    while s < CHUNK:
        last = 2 * s >= CHUNK
        prods = [_mm_hi([t] if last else [t, p], p, expand) for t, p in zip(ts, ps)]
        ts = [t + pr[0] for t, pr in zip(ts, prods)]
        if not last:
            ps = [pr[1] for pr in prods]
        s *= 2
    return ts


def _gdn_kernel(q_ref, k_ref, v_ref, z_ref, ba_ref, cw_ref, alog_ref, dtb_ref, onw_ref,
                o_ref, s_ref, prev_ref):
    c = pl.program_id(1)
    nh, w = GDN_HEADS, GDN_HEADS * HEAD
    heads = range(nh)
    nrows = q_ref.shape[0]
    chunks = [slice(r, r + CHUNK) for r in range(0, nrows, CHUNK)]

    @pl.when(c == 0)
    def _():
        s_ref[...] = jnp.zeros_like(s_ref)
        prev_ref[...] = jnp.zeros_like(prev_ref)

    x = jnp.concatenate([q_ref[...], k_ref[...], v_ref[...]], axis=1)
    prev = prev_ref[...]
    cw = cw_ref[...]
    ntap = cw.shape[0]
    conv = None
    for j in range(ntap):
        s = ntap - 1 - j
        term = (x if s == 0 else _shift_rows(x, prev, s)) * cw[j:j + 1, :]
        conv = term if conv is None else conv + term
    prev_ref[...] = x[nrows - SUBLANES:, :]
    qkv = _silu(conv)

    ba = ba_ref[...]
    beta_t = jax.nn.sigmoid(ba)
    g_t = -jnp.exp(alog_ref[...]) * _softplus(ba + dtb_ref[...])
    gc_ts = [_cumsum_rows(g_t[rs]) for rs in chunks]
    gc_rows = [g.T for g in gc_ts]

    row = lax.broadcasted_iota(jnp.int32, (CHUNK, CHUNK), 0)
    col = lax.broadcasted_iota(jnp.int32, (CHUNK, CHUNK), 1)
    causal = row >= col
    strict = row > col
    z = z_ref[...]
    onw = onw_ref[...]

    def l2(t):
        return t * lax.rsqrt(jnp.sum(t * t, axis=-1, keepdims=True) + EPS)

    lanes = [slice(h * HEAD, (h + 1) * HEAD) for h in heads]
    qn = [l2(qkv[:, h * HEAD:(h + 1) * HEAD]) * (HEAD ** -0.5) for h in heads]
    kn = [l2(qkv[:, w + h * HEAD:w + (h + 1) * HEAD]) for h in heads]

    combos = [(i, h) for i in range(len(chunks)) for h in heads]
    duos = [(p, p + 1) for p in range(0, len(combos), 2)]
    qs = [qn[h][chunks[i]] for i, h in combos]
    ks = [kn[h][chunks[i]] for i, h in combos]
    vs = [qkv[chunks[i], 2 * w + h * HEAD:2 * w + (h + 1) * HEAD] for i, h in combos]
    betas = [beta_t[chunks[i], h:h + 1] for i, h in combos]
    gcols = [gc_ts[i][:, nh + h:nh + h + 1] for i, h in combos]
    grows = [gc_rows[i][nh + h:nh + h + 1, :] for i, h in combos]
    glasts = [g[CHUNK - 1:CHUNK, :] for g in gcols]
    kbs = [k * b for k, b in zip(ks, betas)]
    vbs = [v * b for v, b in zip(vs, betas)]
    kgs = [kb * jnp.exp(gc) for kb, gc in zip(kbs, gcols)]

    row2 = lax.broadcasted_iota(jnp.int32, (CHUNK, 2 * CHUNK), 0)
    lane2 = lax.broadcasted_iota(jnp.int32, (CHUNK, 2 * CHUNK), 1)
    left = lane2 < CHUNK
    col2 = lane2 & (CHUNK - 1)
    causal2 = row2 >= col2
    strict2 = row2 > col2
    eye2 = jnp.where(row2 == col2, 1.0, 0.0)
    zeros = jnp.zeros((CHUNK, HEAD), F32)

    def side_by_side(xa, xb):
        return jnp.concatenate([xa, xb], axis=1)

    def block_diag(t):
        return jnp.concatenate([jnp.where(left, t, 0), jnp.where(left, 0, t)], axis=0)

    decays2 = [jnp.where(causal2,
                         jnp.exp(jnp.where(causal2, jnp.where(left, gcols[a], gcols[b])
                                           - side_by_side(grows[a], grows[b]), 0.0)), 0.0)
               for a, b in duos]
    kq2 = [_mm_nt(jnp.concatenate([side_by_side(kbs[a], kbs[b]), side_by_side(qs[a], qs[b])], axis=0),
                  jnp.concatenate([side_by_side(ks[a], zeros), side_by_side(zeros, ks[b])], axis=0))
           for a, b in duos]
    lows2 = [jnp.where(strict2, kq[:CHUNK] * d, 0.0) for kq, d in zip(kq2, decays2)]
    a_qk2 = [(kq[CHUNK:] * d).astype(BF16) for kq, d in zip(kq2, decays2)]
    tinv2 = _unit_lower_inverses(lows2, eye2, block_diag)
    uw2 = [_mm(block_diag(t.astype(BF16)),
               jnp.concatenate([side_by_side(vbs[a], kgs[a]), side_by_side(vbs[b], kgs[b])], axis=0))
           for t, (a, b) in zip(tinv2, duos)]
    us = [uw[r:r + CHUNK, :HEAD] for uw in uw2 for r in (0, CHUNK)]
    ws = [uw[r:r + CHUNK, HEAD:] for uw in uw2 for r in (0, CHUNK)]
    q_decs = [q * jnp.exp(gc) for q, gc in zip(qs, gcols)]
    k_decs = [k * jnp.exp(gl - gc) for k, gl, gc in zip(ks, glasts, gcols)]
    wqs = [jnp.concatenate([wm, qd], axis=0) for wm, qd in zip(ws, q_decs)]

    states = [s_ref[h] for h in heads]
    for i, rs in enumerate(chunks):
        ids = [i * nh + h for h in heads]
        wq_ss = [_mm(wqs[p], st) for p, st in zip(ids, states)]
        v_news = [us[p] - wq[:CHUNK] for p, wq in zip(ids, wq_ss)]
        intra = [_mm(block_diag(a_qk2[p // 2]), jnp.concatenate([v_news[h], v_news[h + 1]], axis=0))
                 for p, h in zip(ids[::2], heads[::2])]
        outs = [wq[CHUNK:] + intra[h // 2][(h % 2) * CHUNK:(h % 2 + 1) * CHUNK] for h, wq in zip(heads, wq_ss)]
        states = [st * jnp.exp(glasts[p]) + _mm_tn(k_decs[p], vn) for p, st, vn in zip(ids, states, v_news)]
        for h in heads:
            o_ref[rs, lanes[h]] = _rms(outs[h], onw) * _silu(z[rs, lanes[h]])
    for h in heads:
        s_ref[h] = states[h]


def _gdn(proj, conv_w, a_log, dt_bias, out_norm, batch, seq):
    n = proj.shape[0]
    nh, w = GDN_HEADS, GDN_HEADS * HEAD
    rows = GDN_CHUNKS * CHUNK
    steps = seq // rows
    gate_block = 4 * w // HEAD
    pad = lambda p: jnp.zeros((1, HEAD), F32).at[0, nh:2 * nh].set(p.astype(F32))
    at = lambda b, c: b * steps + c
    return pl.pallas_call(
        _gdn_kernel,
        grid=(batch, steps),
        in_specs=[
            pl.BlockSpec((rows, w), lambda b, c: (at(b, c), 0)),
            pl.BlockSpec((rows, w), lambda b, c: (at(b, c), 1)),
            pl.BlockSpec((rows, w), lambda b, c: (at(b, c), 2)),
            pl.BlockSpec((rows, w), lambda b, c: (at(b, c), 3)),
            pl.BlockSpec((rows, HEAD), lambda b, c: (at(b, c), gate_block)),
            pl.BlockSpec(conv_w.shape, lambda b, c: (0, 0)),
            pl.BlockSpec((1, HEAD), lambda b, c: (0, 0)),
            pl.BlockSpec((1, HEAD), lambda b, c: (0, 0)),
            pl.BlockSpec((1, HEAD), lambda b, c: (0, 0)),
        ],
        out_specs=pl.BlockSpec((rows, w), lambda b, c: (at(b, c), 0)),
        out_shape=jax.ShapeDtypeStruct((n, w), F32),
        scratch_shapes=[pltpu.VMEM((nh, HEAD, HEAD), F32), pltpu.VMEM((SUBLANES, 3 * w), F32)],
        compiler_params=pltpu.CompilerParams(
            dimension_semantics=("parallel", "arbitrary"), vmem_limit_bytes=VMEM_LIMIT),
        name="gdn",
    )(proj, proj, proj, proj, proj, conv_w, pad(a_log), pad(dt_bias), out_norm.reshape(1, HEAD))


def _sb_kernel(q_ref, k_ref, v_ref, uo_ref, o_ref, surv_ref):
    base = pl.program_id(2) * SB_TILES
    uo = uo_ref[...]
    row = lax.broadcasted_iota(jnp.int32, (SB_BLOCK, SB_BLOCK), 0)
    col = lax.broadcasted_iota(jnp.int32, (SB_BLOCK, SB_BLOCK), 1)
    earlier = col < row

    def on_diagonal(t):
        top = jnp.where(earlier, t[:SB_BLOCK], 0.0)
        return top if t.shape[0] == SB_BLOCK else jnp.concatenate([top, t[SB_BLOCK:]], axis=0)

    def block(j):
        return pl.ds(pl.multiple_of(j * SB_BLOCK, SB_BLOCK), SB_BLOCK)

    def score(j, r0, diagonal):
        z = _mm_nt(q_ref[r0:, :], k_ref[block(j), :])
        log_beta = _log_sigmoid(z)
        log_fail = log_beta - z
        if diagonal:
            log_fail = on_diagonal(log_fail)
        hi, lo = _split(log_fail)
        sums = jnp.dot(jnp.concatenate([hi, lo], axis=1), uo, preferred_element_type=F32)
        return log_beta + sums[:, :SB_BLOCK], sums[:, SB_BLOCK:]

    def absorb(j, r0, diagonal, log_w, total):
        surv = surv_ref[r0:, :]
        wts = jnp.exp(log_w + surv)
        if diagonal:
            wts = on_diagonal(wts)
        o_ref[r0:, :] += _mm(wts, v_ref[block(j), :])
        surv = surv + total
        surv_ref[r0:, :] = surv
        return surv

    o_ref[...] = jnp.zeros_like(o_ref)
    surv_ref[...] = jnp.zeros_like(surv_ref)
    own = [(base + t, t * SB_BLOCK) for t in reversed(range(SB_TILES))]
    scored = [score(j, r0, True) for j, r0 in own]

    def score_two_earlier(j):
        out = []
        for jj in (j, j - 1):
            log_w, total = score(jnp.maximum(jj, 0), 0, False)
            out.append((jnp.where(jj >= 0, log_w, -jnp.inf), jnp.where(jj >= 0, total, 0.0)))
        return out

    def absorb_two_earlier(j, two):
        for jj, (log_w, total) in zip((j, j - 1), two):
            surv = absorb(jnp.maximum(jj, 0), 0, False, log_w, total)
        return surv

    before = score_two_earlier(base - 1)
    for (j, r0), (log_w, total) in zip(own, scored):
        absorb(j, r0, True, log_w, total)
    surv = absorb_two_earlier(base - 1, before)

    def more(c):
        j, top = c
        return jnp.logical_and(j >= 0, top > SURV_FLOOR)

    def earlier_blocks(c):
        j, _ = c
        return j - 2, jnp.max(absorb_two_earlier(j, score_two_earlier(j)))

    lax.while_loop(more, earlier_blocks, (base - 3, jnp.max(surv)))


def _stick_breaking(qkv, batch, seq):
    n = qkv.shape[0]
    nh = SB_HEADS
    tq = SB_TILES * SB_BLOCK
    nq = seq // tq
    kk = lax.broadcasted_iota(jnp.int32, (SB_BLOCK, 2 * SB_BLOCK), 0)
    cc = lax.broadcasted_iota(jnp.int32, (SB_BLOCK, 2 * SB_BLOCK), 1)
    uo = jnp.where((kk > cc) | (cc >= SB_BLOCK), 1.0, 0.0).astype(BF16)
    uo = jnp.concatenate([uo, uo], axis=0)
    return pl.pallas_call(
        _sb_kernel,
        grid=(batch, nh, nq),
        in_specs=[
            pl.BlockSpec((tq, HEAD), lambda b, h, i: (b * nq + i, h)),
            pl.BlockSpec((seq, HEAD), lambda b, h, i: (b, nh + h)),
            pl.BlockSpec((seq, HEAD), lambda b, h, i: (b, 2 * nh + h)),
            pl.BlockSpec(uo.shape, lambda b, h, i: (0, 0)),
        ],
        out_specs=pl.BlockSpec((tq, HEAD), lambda b, h, i: (b * nq + i, h)),
        out_shape=jax.ShapeDtypeStruct((n, nh * HEAD), F32),
        scratch_shapes=[pltpu.VMEM((tq, HEAD), F32)],
        compiler_params=pltpu.CompilerParams(
            dimension_semantics=("parallel", "parallel", "arbitrary"), vmem_limit_bytes=VMEM_LIMIT),
        name="stick_breaking",
    )(qkv, qkv, qkv, uo)


def _hgrn_kernel(q_ref, f_ref, i_ref, g_ref, lbp_ref, onw_ref, sel_ref, o_ref, st_ref, *, layer):
    c = pl.program_id(2)

    @pl.when(c == 0)
    def _():
        st_ref[...] = jnp.zeros_like(st_ref)

    lbp = lbp_ref[...]
    e = jnp.exp(lbp - jnp.max(lbp, axis=0, keepdims=True))
    soft = e / jnp.sum(e, axis=0, keepdims=True)
    lb = soft[0:1, :]
    for l in range(1, layer + 1):
        lb = lb + soft[l:l + 1, :]
    lb = lb - soft[0:1, :]

    ff = f_ref[...]
    t = jnp.exp(-jnp.abs(ff))
    a = jnp.log(lb)
    b = jnp.log(1.0 - lb) + (jnp.minimum(ff, 0.0) - jnp.log(1.0 + t))
    log_f = jnp.maximum(a, b) + _log1p_exp(a - b)
    kk = (1.0 - lb) * (jnp.where(ff >= 0.0, t, 1.0) / (1.0 + t))
    qq = _silu(q_ref[...])
    vv = i_ref[...]
    gate = g_ref[...]
    onw = onw_ref[...]

    chunks = [slice(r, r + CHUNK) for r in range(0, q_ref.shape[0], CHUNK)]
    hi, lo = _split(log_f)
    runs = [jnp.dot(sel_ref[...], jnp.concatenate([hi[rs], lo[rs]], axis=0), preferred_element_type=F32)
            for rs in chunks]

    def block(b):
        return jnp.concatenate([r[b * CHUNK:(b + 1) * CHUNK] for r in runs], axis=0)

    q_dec = (qq * jnp.exp(block(0))).astype(BF16)
    k_dec = (kk * jnp.exp(block(1))).astype(BF16)
    g_lasts = [jnp.exp(r[CHUNK - 1:CHUNK, :]) for r in runs]

    rrow = lax.broadcasted_iota(jnp.int32, qq.shape, 0)
    heads = range(q_ref.shape[1] // HEAD)
    lanes = [slice(h * HEAD, (h + 1) * HEAD) for h in heads]
    lanes2 = [slice(h * HEAD, (h + 2) * HEAD) for h in heads[::2]]
    duos = [(rs, l2) for rs in chunks for l2 in lanes2]
    row2 = lax.broadcasted_iota(jnp.int32, (CHUNK, 2 * CHUNK), 0)
    lane2 = lax.broadcasted_iota(jnp.int32, (CHUNK, 2 * CHUNK), 1)
    left = lane2 < CHUNK
    col2 = lane2 & (CHUNK - 1)
    first = lax.broadcasted_iota(jnp.int32, (CHUNK, 2 * HEAD), 1) < HEAD

    def key_blocks(t):
        return jnp.concatenate([jnp.where(first, t, 0), jnp.where(first, 0, t)], axis=0)

    def score_blocks(t):
        return jnp.concatenate([jnp.where(left, t, 0), jnp.where(left, 0, t)], axis=0)

    qk = qq * kk
    scores = [jnp.where(row2 == col2,
                        jnp.where(left, jnp.sum(qk[rs, l2][:, :HEAD], axis=1, keepdims=True),
                                  jnp.sum(qk[rs, l2][:, HEAD:], axis=1, keepdims=True)), 0.0)
              for rs, l2 in duos]
    s, level = CHUNK, 2
    while s > 1:
        upper = (rrow & (s - 1)) >= s // 2
        same = (row2 & -s) == (col2 & -s)
        factor = jnp.exp(block(level))
        qt = jnp.where(upper, qq * factor, 0.0).astype(BF16)
        kt = jnp.where(upper, 0.0, kk * factor).astype(BF16)
        scores = [a + jnp.where(same, _mm_nt(qt[rs, l2], key_blocks(kt[rs, l2])), 0.0)
                  for a, (rs, l2) in zip(scores, duos)]
        s, level = s // 2, level + 1
    scores = [score_blocks(a.astype(BF16)) for a in scores]

    states = [st_ref[h] for h in heads]
    for i, rs in enumerate(chunks):
        intra = [_mm(scores[i * len(lanes2) + d], jnp.concatenate([vv[rs, lanes[2 * d]], vv[rs, lanes[2 * d + 1]]], axis=0))
                 for d in range(len(lanes2))]
        outs = [_mm_nt(q_dec[rs, ln], st) + intra[h // 2][(h % 2) * CHUNK:(h % 2 + 1) * CHUNK]
                for h, (ln, st) in enumerate(zip(lanes, states))]
        states = [st * g_lasts[i][:, ln] + _mm_tn(vv[rs, ln], k_dec[rs, ln]) for ln, st in zip(lanes, states)]
        for h, ln in enumerate(lanes):
            o_ref[rs, ln] = _rms(outs[h], onw) * _silu(gate[rs, ln])
    for h in heads:
        st_ref[h] = states[h]


def _run_selectors():
    i = lax.broadcasted_iota(jnp.int32, (CHUNK, CHUNK), 0)
    r = lax.broadcasted_iota(jnp.int32, (CHUNK, CHUNK), 1)
    blocks = [r <= i, r > i]
    s = CHUNK
    while s > 1:
        mid = (i & -s) + s // 2
        blocks.append(jnp.where(i >= mid, (r > mid) & (r <= i), (r > i) & (r <= mid)))
        s //= 2
    sel = jnp.concatenate(blocks, axis=0).astype(BF16)
    return jnp.concatenate([sel, sel], axis=1)


def _hgrn(proj, lower_bounds, layer, out_norm, batch, seq):
    n = proj.shape[0]
    w = proj.shape[1] // 4
    gw = HGRN_GROUP * HEAD
    ng = w // gw
    tr = HGRN_CHUNKS * CHUNK
    nc = seq // tr
    sel = _run_selectors()
    rows = lambda b, c: b * nc + c
    return pl.pallas_call(
        functools.partial(_hgrn_kernel, layer=layer),
        grid=(batch, ng, nc),
        in_specs=[
            pl.BlockSpec((tr, gw), lambda b, g, c: (rows(b, c), g)),
            pl.BlockSpec((tr, gw), lambda b, g, c: (rows(b, c), ng + g)),
            pl.BlockSpec((tr, gw), lambda b, g, c: (rows(b, c), 2 * ng + g)),
            pl.BlockSpec((tr, gw), lambda b, g, c: (rows(b, c), 3 * ng + g)),
            pl.BlockSpec((lower_bounds.shape[0], gw), lambda b, g, c: (0, g)),
            pl.BlockSpec((1, HEAD), lambda b, g, c: (0, 0)),
            pl.BlockSpec(sel.shape, lambda b, g, c: (0, 0)),
        ],
        out_specs=pl.BlockSpec((tr, gw), lambda b, g, c: (rows(b, c), g)),
        out_shape=jax.ShapeDtypeStruct((n, w), F32),
        scratch_shapes=[pltpu.VMEM((HGRN_GROUP, HEAD, HEAD), F32)],
        compiler_params=pltpu.CompilerParams(
            dimension_semantics=("parallel", "parallel", "arbitrary"), vmem_limit_bytes=VMEM_LIMIT),
        name="hgrn2",
    )(proj, proj, proj, proj, lower_bounds.astype(F32), out_norm.reshape(1, HEAD), sel)


def _even_mixer(h, norm_w, w_in, conv_w, a_log, dt_bias, out_norm, w_out, batch, seq):
    gw, sw, nh = GDN_HEADS * HEAD, SB_HEADS * HEAD, GDN_HEADS
    o1 = 4 * gw
    o2 = o1 + 2 * nh
    w_gdn = jnp.concatenate([w_in[:, :o1], w_in[:, o1:o2],
                             jnp.zeros((w_in.shape[0], HEAD - 2 * nh), w_in.dtype)], axis=1)
    p_gdn = _norm_proj(h, norm_w, w_gdn, F32, tn=11 * HEAD)
    scale = jnp.concatenate([jnp.full((sw,), HEAD ** -0.5, F32), jnp.ones((2 * sw,), F32)])
    p_sb = _norm_proj(h, norm_w, w_in[:, o2:], BF16, tn=8 * HEAD, col_scale=scale)
    o_a = _gdn(p_gdn, conv_w, a_log, dt_bias, out_norm, batch, seq)
    o_b = _stick_breaking(p_sb, batch, seq)
    return _out_proj(h, [o_a, o_b], [w_out[:gw], w_out[gw:]])


def _odd_mixer(h, norm_w, w_in, lower_bounds, layer, out_norm, w_out, batch, seq):
    proj = _norm_proj(h, norm_w, w_in, F32, tn=8 * HEAD)
    o = _hgrn(proj, lower_bounds, layer, out_norm, batch, seq)
    return _out_proj(h, [o], [w_out])


def kernel(x, ffn1_norm, ffn1_w_gate, ffn1_w_up, ffn1_w_down, mix_norm, ffn2_norm, ffn2_w_gate, ffn2_w_up, ffn2_w_down, even_w_in, gdn_conv_w, gdn_a_log, gdn_dt_bias, gdn_out_norm, even_w_out, odd_w_in, hgrn_lower_bounds, hgrn_out_norm, odd_w_out, final_norm):
    batch, seq, d = x.shape
    depth = ffn1_norm.shape[0]
    h = x.reshape(batch * seq, d)
    for layer in range(depth):
        h = _ffn(h, ffn1_norm[layer], ffn1_w_gate[layer], ffn1_w_up[layer], ffn1_w_down[layer])
        m = layer // 2
        if layer % 2 == 0:
            h = _even_mixer(h, mix_norm[layer], even_w_in[m], gdn_conv_w[m], gdn_a_log[m], gdn_dt_bias[m],
                            gdn_out_norm[m], even_w_out[m], batch, seq)
        else:
            h = _odd_mixer(h, mix_norm[layer], odd_w_in[m], hgrn_lower_bounds, layer, hgrn_out_norm[m],
                           odd_w_out[m], batch, seq)
        h = _ffn(h, ffn2_norm[layer], ffn2_w_gate[layer], ffn2_w_up[layer], ffn2_w_down[layer],
                 final_w=final_norm if layer == depth - 1 else None)
    return h.reshape(batch, seq, d)
```

```python
import functools
import math

import jax
import jax.numpy as jnp
from jax import lax
from jax.experimental import pallas as pl
from jax.experimental.pallas import tpu as pltpu

F32 = jnp.float32
BF16 = jnp.bfloat16
EPS = 1e-6

HEAD = 128
GDN_HEADS = 8
SB_HEADS = 8
CHUNK = 64
GDN_CHUNKS = 8
SB_BLOCK = 128
SB_TILES = 4
SURV_FLOOR = -105.0
SUBLANES = 8

FFN_ROWS = 1024
FFN_COLS = 512
FFN_SLAB = 256
FFN_VMEM_LIMIT = 60 * 1024 * 1024
PROJ_ROWS = 1024
OUT_ROWS = 512
HGRN_GROUP = 8
HGRN_CHUNKS = 8
VMEM_LIMIT = 48 * 1024 * 1024


def _mm(a, b):
    return jnp.dot(a.astype(BF16), b.astype(BF16), preferred_element_type=F32)


def _mm_nt(a, b):
    return lax.dot_general(a.astype(BF16), b.astype(BF16), (((1,), (1,)), ((), ())),
                           preferred_element_type=F32)


def _mm_tn(a, b):
    return lax.dot_general(a.astype(BF16), b.astype(BF16), (((0,), (0,)), ((), ())),
                           preferred_element_type=F32)


def _split(x):
    hi = x.astype(BF16)
    lo = (x - hi.astype(F32)).astype(BF16)
    return hi, lo


def _mm_hi(lhs, b, expand):
    m, n = lhs[0].shape[0], len(lhs)
    his, los = zip(*[_split(a) for a in lhs])
    bh, bl = (expand(t) for t in _split(b))
    top = jnp.dot(jnp.concatenate(his + los, axis=0), bh, preferred_element_type=F32)
    low = jnp.dot(jnp.concatenate(his, axis=0), bl, preferred_element_type=F32)
    rows = lambda t, i: t[i * m:(i + 1) * m]
    return [rows(top, i) + (rows(low, i) + rows(top, n + i)) for i in range(n)]


def _rms(x, w):
    return x * lax.rsqrt(jnp.mean(x * x, axis=-1, keepdims=True) + EPS) * w


def _silu(x):
    return x * jax.nn.sigmoid(x)


def _log1p_exp(x):
    return jnp.log(1.0 + jnp.exp(-jnp.abs(x)))


def _softplus(x):
    return jnp.maximum(x, 0.0) + _log1p_exp(x)


def _log_sigmoid(x):
    return jnp.minimum(x, 0.0) - _log1p_exp(x)


def _cumsum_rows(x):
    row = lax.broadcasted_iota(jnp.int32, x.shape, 0)
    s = 1
    while s < x.shape[0]:
        x = x + jnp.where(row >= s, pltpu.roll(x, s, 0), 0.0)
        s *= 2
    return x


def _shift_rows(x, prev, s):
    from_above = lax.broadcasted_iota(jnp.int32, prev.shape, 0) < s
    tiles = [prev] + [x[r:r + SUBLANES] for r in range(0, x.shape[0], SUBLANES)]
    turned = [pltpu.roll(t, s, 0) for t in tiles]
    return jnp.concatenate([jnp.where(from_above, above, here)
                            for above, here in zip(turned[:-1], turned[1:])], axis=0)


def _ffn_kernel(x_ref, nw_ref, wg_ref, wu_ref, wd_ref, fw_ref, o_ref, xn_ref, *, final_norm):
    j = pl.program_id(1)
    tm, d = x_ref.shape
    col_slabs = [slice(n, n + FFN_SLAB) for n in range(0, d, FFN_SLAB)]

    def for_row_slabs(body):
        def step(r, carry):
            body(pl.ds(pl.multiple_of(r * FFN_SLAB, FFN_SLAB), FFN_SLAB))
            return carry
        lax.fori_loop(0, tm // FFN_SLAB, step, 0)

    def hidden_step(first):
        xn = xn_ref[...]
        g = jnp.dot(xn, wg_ref[...], preferred_element_type=F32)
        u = jnp.dot(xn, wu_ref[...], preferred_element_type=F32)
        act = (_silu(g) * u).astype(BF16)
        for cs in col_slabs:
            part = jnp.dot(act, wd_ref[:, cs], preferred_element_type=F32)
            o_ref[:, cs] = part if first else o_ref[:, cs] + part

    @pl.when(j == 0)
    def _():
        def normalise(rs):
            xn_ref[rs, :] = _rms(x_ref[rs, :], nw_ref[...]).astype(BF16)
        for_row_slabs(normalise)
        hidden_step(first=True)

    @pl.when(j > 0)
    def _():
        hidden_step(first=False)

    @pl.when(j == pl.num_programs(1) - 1)
    def _():
        def finish(rs):
            y = x_ref[rs, :] + 0.5 * o_ref[rs, :]
            if final_norm:
                y = _rms(y, fw_ref[...])
            o_ref[rs, :] = y
        for_row_slabs(finish)


def _ffn(h, norm_w, w_gate, w_up, w_down, final_w=None):
    n, d = h.shape
    f = w_gate.shape[1]
    tm, tf = min(FFN_ROWS, n), FFN_COLS
    fw = norm_w if final_w is None else final_w
    return pl.pallas_call(
        functools.partial(_ffn_kernel, final_norm=final_w is not None),
        grid=(n // tm, f // tf),
        in_specs=[
            pl.BlockSpec((tm, d), lambda i, j: (i, 0)),
            pl.BlockSpec((1, d), lambda i, j: (0, 0)),
            pl.BlockSpec((d, tf), lambda i, j: (0, j)),
            pl.BlockSpec((d, tf), lambda i, j: (0, j)),
            pl.BlockSpec((tf, d), lambda i, j: (j, 0)),
            pl.BlockSpec((1, d), lambda i, j: (0, 0)),
        ],
        out_specs=pl.BlockSpec((tm, d), lambda i, j: (i, 0)),
        out_shape=jax.ShapeDtypeStruct((n, d), F32),
        scratch_shapes=[pltpu.VMEM((tm, d), BF16)],
        compiler_params=pltpu.CompilerParams(
            dimension_semantics=("parallel", "arbitrary"), vmem_limit_bytes=FFN_VMEM_LIMIT),
        name="ffn",
    )(h, norm_w.reshape(1, d), w_gate.astype(BF16), w_up.astype(BF16), w_down.astype(BF16),
      fw.reshape(1, d))


def _norm_proj_kernel(x_ref, nw_ref, w_ref, cs_ref, o_ref, xn_ref, *, scaled):
    @pl.when(pl.program_id(1) == 0)
    def _():
        xn_ref[...] = _rms(x_ref[...], nw_ref[...]).astype(BF16)

    y = jnp.dot(xn_ref[...], w_ref[...], preferred_element_type=F32)
    if scaled:
        y = y * cs_ref[...]
    o_ref[...] = y.astype(o_ref.dtype)


def _norm_proj(h, norm_w, w, out_dtype, tn, col_scale=None):
    n, d = h.shape
    c = w.shape[1]
    tm = min(PROJ_ROWS, n)
    cs = jnp.ones((1, c), F32) if col_scale is None else col_scale.reshape(1, c)
    return pl.pallas_call(
        functools.partial(_norm_proj_kernel, scaled=col_scale is not None),
        grid=(n // tm, c // tn),
        in_specs=[
            pl.BlockSpec((tm, d), lambda i, j: (i, 0)),
            pl.BlockSpec((1, d), lambda i, j: (0, 0)),
            pl.BlockSpec((d, tn), lambda i, j: (0, j)),
            pl.BlockSpec((1, tn), lambda i, j: (0, j)),
        ],
        out_specs=pl.BlockSpec((tm, tn), lambda i, j: (i, j)),
        out_shape=jax.ShapeDtypeStruct((n, c), out_dtype),
        scratch_shapes=[pltpu.VMEM((tm, d), BF16)],
        compiler_params=pltpu.CompilerParams(
            dimension_semantics=("parallel", "arbitrary"), vmem_limit_bytes=FFN_VMEM_LIMIT),
        name="norm_proj",
    )(h, norm_w.reshape(1, d), w.astype(BF16), cs)


def _out_proj_kernel(*refs):
    h_ref, o_ref = refs[0], refs[-1]
    k = (len(refs) - 2) // 2
    acc = h_ref[...]
    for x_ref, w_ref in zip(refs[1:1 + k], refs[1 + k:1 + 2 * k]):
        acc = acc + _mm(x_ref[...], w_ref[...])
    o_ref[...] = acc


def _out_proj(h, xs, ws):
    n, d = h.shape
    tm = min(OUT_ROWS, n)
    return pl.pallas_call(
        _out_proj_kernel,
        grid=(n // tm,),
        in_specs=([pl.BlockSpec((tm, d), lambda i: (i, 0))]
                  + [pl.BlockSpec((tm, x.shape[1]), lambda i: (i, 0)) for x in xs]
                  + [pl.BlockSpec(w.shape, lambda i: (0, 0)) for w in ws]),
        out_specs=pl.BlockSpec((tm, d), lambda i: (i, 0)),
        out_shape=jax.ShapeDtypeStruct((n, d), F32),
        compiler_params=pltpu.CompilerParams(
            dimension_semantics=("parallel",), vmem_limit_bytes=VMEM_LIMIT),
        name="out_proj",
    )(h, *xs, *[w.astype(BF16) for w in ws])


def _unit_lower_inverses(lows, eye, expand):
    ps = [-low for low in lows]
    ts = [eye + p for p in ps]
    ps = [_mm_hi([p], p, expand)[0] for p in ps]
    s = 2
    while s < CHUNK:
        last = 2 * s >= CHUNK
        prods = [_mm_hi([t] if last else [t, p], p, expand) for t, p in zip(ts, ps)]
        ts = [t + pr[0] for t, pr in zip(ts, prods)]
        if not last:
            ps = [pr[1] for pr in prods]
        s *= 2
    return ts


def _gdn_kernel(q_ref, k_ref, v_ref, z_ref, ba_ref, cw_ref, alog_ref, dtb_ref, onw_ref,
                o_ref, s_ref, prev_ref):
    c = pl.program_id(1)
    nh, w = GDN_HEADS, GDN_HEADS * HEAD
    heads = range(nh)
    nrows = q_ref.shape[0]
    chunks = [slice(r, r + CHUNK) for r in range(0, nrows, CHUNK)]

    @pl.when(c == 0)
    def _():
        s_ref[...] = jnp.zeros_like(s_ref)
        prev_ref[...] = jnp.zeros_like(prev_ref)

    x = jnp.concatenate([q_ref[...], k_ref[...], v_ref[...]], axis=1)
    prev = prev_ref[...]
    cw = cw_ref[...]
    ntap = cw.shape[0]
    conv = None
    for j in range(ntap):
        s = ntap - 1 - j
        term = (x if s == 0 else _shift_rows(x, prev, s)) * cw[j:j + 1, :]
        conv = term if conv is None else conv + term
    prev_ref[...] = x[nrows - SUBLANES:, :]
    qkv = _silu(conv)

    ba = ba_ref[...]
    beta_t = jax.nn.sigmoid(ba)
    g_t = -jnp.exp(alog_ref[...]) * _softplus(ba + dtb_ref[...])
    gc_ts = [_cumsum_rows(g_t[rs]) for rs in chunks]
    gc_rows = [g.T for g in gc_ts]

    row = lax.broadcasted_iota(jnp.int32, (CHUNK, CHUNK), 0)
    col = lax.broadcasted_iota(jnp.int32, (CHUNK, CHUNK), 1)
    causal = row >= col
    strict = row > col
    z = z_ref[...]
    onw = onw_ref[...]

    def l2(t):
        return t * lax.rsqrt(jnp.sum(t * t, axis=-1, keepdims=True) + EPS)

    lanes = [slice(h * HEAD, (h + 1) * HEAD) for h in heads]
    qn = [l2(qkv[:, h * HEAD:(h + 1) * HEAD]) * (HEAD ** -0.5) for h in heads]
    kn = [l2(qkv[:, w + h * HEAD:w + (h + 1) * HEAD]) for h in heads]

    combos = [(i, h) for i in range(len(chunks)) for h in heads]
    duos = [(p, p + 1) for p in range(0, len(combos), 2)]
    qs = [qn[h][chunks[i]] for i, h in combos]
    ks = [kn[h][chunks[i]] for i, h in combos]
    vs = [qkv[chunks[i], 2 * w + h * HEAD:2 * w + (h + 1) * HEAD] for i, h in combos]
    betas = [beta_t[chunks[i], h:h + 1] for i, h in combos]
    gcols = [gc_ts[i][:, nh + h:nh + h + 1] for i, h in combos]
    grows = [gc_rows[i][nh + h:nh + h + 1, :] for i, h in combos]
    glasts = [g[CHUNK - 1:CHUNK, :] for g in gcols]
    kbs = [k * b for k, b in zip(ks, betas)]
    vbs = [v * b for v, b in zip(vs, betas)]
    kgs = [kb * jnp.exp(gc) for kb, gc in zip(kbs, gcols)]

    row2 = lax.broadcasted_iota(jnp.int32, (CHUNK, 2 * CHUNK), 0)
    lane2 = lax.broadcasted_iota(jnp.int32, (CHUNK, 2 * CHUNK), 1)
    left = lane2 < CHUNK
    col2 = lane2 & (CHUNK - 1)
    causal2 = row2 >= col2
    strict2 = row2 > col2
    eye2 = jnp.where(row2 == col2, 1.0, 0.0)
    zeros = jnp.zeros((CHUNK, HEAD), F32)

    def side_by_side(xa, xb):
        return jnp.concatenate([xa, xb], axis=1)

    def block_diag(t):
        return jnp.concatenate([jnp.where(left, t, 0), jnp.where(left, 0, t)], axis=0)

    decays2 = [jnp.where(causal2,
                         jnp.exp(jnp.where(causal2, jnp.where(left, gcols[a], gcols[b])
                                           - side_by_side(grows[a], grows[b]), 0.0)), 0.0)
               for a, b in duos]
    kq2 = [_mm_nt(jnp.concatenate([side_by_side(kbs[a], kbs[b]), side_by_side(qs[a], qs[b])], axis=0),
                  jnp.concatenate([side_by_side(ks[a], zeros), side_by_side(zeros, ks[b])], axis=0))
           for a, b in duos]
    lows2 = [jnp.where(strict2, kq[:CHUNK] * d, 0.0) for kq, d in zip(kq2, decays2)]
    a_qk2 = [(kq[CHUNK:] * d).astype(BF16) for kq, d in zip(kq2, decays2)]
    tinv2 = _unit_lower_inverses(lows2, eye2, block_diag)
    uw2 = [_mm(block_diag(t.astype(BF16)),
               jnp.concatenate([side_by_side(vbs[a], kgs[a]), side_by_side(vbs[b], kgs[b])], axis=0))
           for t, (a, b) in zip(tinv2, duos)]
    us = [uw[r:r + CHUNK, :HEAD] for uw in uw2 for r in (0, CHUNK)]
    ws = [uw[r:r + CHUNK, HEAD:] for uw in uw2 for r in (0, CHUNK)]
    q_decs = [q * jnp.exp(gc) for q, gc in zip(qs, gcols)]
    k_decs = [k * jnp.exp(gl - gc) for k, gl, gc in zip(ks, glasts, gcols)]
    wqs = [jnp.concatenate([wm, qd], axis=0) for wm, qd in zip(ws, q_decs)]

    states = [s_ref[h] for h in heads]
    for i, rs in enumerate(chunks):
        ids = [i * nh + h for h in heads]
        wq_ss = [_mm(wqs[p], st) for p, st in zip(ids, states)]
        v_news = [us[p] - wq[:CHUNK] for p, wq in zip(ids, wq_ss)]
        intra = [_mm(block_diag(a_qk2[p // 2]), jnp.concatenate([v_news[h], v_news[h + 1]], axis=0))
                 for p, h in zip(ids[::2], heads[::2])]
        outs = [wq[CHUNK:] + intra[h // 2][(h % 2) * CHUNK:(h % 2 + 1) * CHUNK] for h, wq in zip(heads, wq_ss)]
        states = [st * jnp.exp(glasts[p]) + _mm_tn(k_decs[p], vn) for p, st, vn in zip(ids, states, v_news)]
        for h in heads:
            o_ref[rs, lanes[h]] = _rms(outs[h], onw) * _silu(z[rs, lanes[h]])
    for h in heads:
        s_ref[h] = states[h]


def _gdn(proj, conv_w, a_log, dt_bias, out_norm, batch, seq):
    n = proj.shape[0]
    nh, w = GDN_HEADS, GDN_HEADS * HEAD
    rows = GDN_CHUNKS * CHUNK
    steps = seq // rows
    gate_block = 4 * w // HEAD
    pad = lambda p: jnp.zeros((1, HEAD), F32).at[0, nh:2 * nh].set(p.astype(F32))
    at = lambda b, c: b * steps + c
    return pl.pallas_call(
        _gdn_kernel,
        grid=(batch, steps),
        in_specs=[
            pl.BlockSpec((rows, w), lambda b, c: (at(b, c), 0)),
            pl.BlockSpec((rows, w), lambda b, c: (at(b, c), 1)),
            pl.BlockSpec((rows, w), lambda b, c: (at(b, c), 2)),
            pl.BlockSpec((rows, w), lambda b, c: (at(b, c), 3)),
            pl.BlockSpec((rows, HEAD), lambda b, c: (at(b, c), gate_block)),
            pl.BlockSpec(conv_w.shape, lambda b, c: (0, 0)),
            pl.BlockSpec((1, HEAD), lambda b, c: (0, 0)),
            pl.BlockSpec((1, HEAD), lambda b, c: (0, 0)),
            pl.BlockSpec((1, HEAD), lambda b, c: (0, 0)),
        ],
        out_specs=pl.BlockSpec((rows, w), lambda b, c: (at(b, c), 0)),
        out_shape=jax.ShapeDtypeStruct((n, w), F32),
        scratch_shapes=[pltpu.VMEM((nh, HEAD, HEAD), F32), pltpu.VMEM((SUBLANES, 3 * w), F32)],
        compiler_params=pltpu.CompilerParams(
            dimension_semantics=("parallel", "arbitrary"), vmem_limit_bytes=VMEM_LIMIT),
        name="gdn",
    )(proj, proj, proj, proj, proj, conv_w, pad(a_log), pad(dt_bias), out_norm.reshape(1, HEAD))


def _sb_kernel(q_ref, k_ref, v_ref, uo_ref, o_ref, surv_ref):
    base = pl.program_id(2) * SB_TILES
    uo = uo_ref[...]
    row = lax.broadcasted_iota(jnp.int32, (SB_BLOCK, SB_BLOCK), 0)
    col = lax.broadcasted_iota(jnp.int32, (SB_BLOCK, SB_BLOCK), 1)
    earlier = col < row

    def on_diagonal(t):
        top = jnp.where(earlier, t[:SB_BLOCK], 0.0)
        return top if t.shape[0] == SB_BLOCK else jnp.concatenate([top, t[SB_BLOCK:]], axis=0)

    def block(j):
        return pl.ds(pl.multiple_of(j * SB_BLOCK, SB_BLOCK), SB_BLOCK)

    def score(j, r0, diagonal):
        z = _mm_nt(q_ref[r0:, :], k_ref[block(j), :])
        log_beta = _log_sigmoid(z)
        log_fail = log_beta - z
        if diagonal:
            log_fail = on_diagonal(log_fail)
        hi, lo = _split(log_fail)
        sums = jnp.dot(jnp.concatenate([hi, lo], axis=1), uo, preferred_element_type=F32)
        return log_beta + sums[:, :SB_BLOCK], sums[:, SB_BLOCK:]

    def absorb(j, r0, diagonal, log_w, total):
        surv = surv_ref[r0:, :]
        wts = jnp.exp(log_w + surv)
        if diagonal:
            wts = on_diagonal(wts)
        o_ref[r0:, :] += _mm(wts, v_ref[block(j), :])
        surv = surv + total
        surv_ref[r0:, :] = surv
        return surv

    o_ref[...] = jnp.zeros_like(o_ref)
    surv_ref[...] = jnp.zeros_like(surv_ref)
    own = [(base + t, t * SB_BLOCK) for t in reversed(range(SB_TILES))]
    scored = [score(j, r0, True) for j, r0 in own]

    def score_two_earlier(j):
        out = []
        for jj in (j, j - 1):
            log_w, total = score(jnp.maximum(jj, 0), 0, False)
            out.append((jnp.where(jj >= 0, log_w, -jnp.inf), jnp.where(jj >= 0, total, 0.0)))
        return out

    def absorb_two_earlier(j, two):
        for jj, (log_w, total) in zip((j, j - 1), two):
            surv = absorb(jnp.maximum(jj, 0), 0, False, log_w, total)
        return surv

    before = score_two_earlier(base - 1)
    for (j, r0), (log_w, total) in zip(own, scored):
        absorb(j, r0, True, log_w, total)
    surv = absorb_two_earlier(base - 1, before)

    def more(c):
        j, top = c
        return jnp.logical_and(j >= 0, top > SURV_FLOOR)

    def earlier_blocks(c):
        j, _ = c
        return j - 2, jnp.max(absorb_two_earlier(j, score_two_earlier(j)))

    lax.while_loop(more, earlier_blocks, (base - 3, jnp.max(surv)))


def _stick_breaking(qkv, batch, seq):
    n = qkv.shape[0]
    nh = SB_HEADS
    tq = SB_TILES * SB_BLOCK
    nq = seq // tq
    kk = lax.broadcasted_iota(jnp.int32, (SB_BLOCK, 2 * SB_BLOCK), 0)
    cc = lax.broadcasted_iota(jnp.int32, (SB_BLOCK, 2 * SB_BLOCK), 1)
    uo = jnp.where((kk > cc) | (cc >= SB_BLOCK), 1.0, 0.0).astype(BF16)
    uo = jnp.concatenate([uo, uo], axis=0)
    return pl.pallas_call(
        _sb_kernel,
        grid=(batch, nh, nq),
        in_specs=[
            pl.BlockSpec((tq, HEAD), lambda b, h, i: (b * nq + i, h)),
            pl.BlockSpec((seq, HEAD), lambda b, h, i: (b, nh + h)),
            pl.BlockSpec((seq, HEAD), lambda b, h, i: (b, 2 * nh + h)),
            pl.BlockSpec(uo.shape, lambda b, h, i: (0, 0)),
        ],
        out_specs=pl.BlockSpec((tq, HEAD), lambda b, h, i: (b * nq + i, h)),
        out_shape=jax.ShapeDtypeStruct((n, nh * HEAD), F32),
        scratch_shapes=[pltpu.VMEM((tq, HEAD), F32)],
        compiler_params=pltpu.CompilerParams(
            dimension_semantics=("parallel", "parallel", "arbitrary"), vmem_limit_bytes=VMEM_LIMIT),
        name="stick_breaking",
    )(qkv, qkv, qkv, uo)


def _hgrn_kernel(q_ref, f_ref, i_ref, g_ref, lbp_ref, onw_ref, sel_ref, o_ref, st_ref, *, layer):
    c = pl.program_id(2)

    @pl.when(c == 0)
    def _():
        st_ref[...] = jnp.zeros_like(st_ref)

    lbp = lbp_ref[...]
    e = jnp.exp(lbp - jnp.max(lbp, axis=0, keepdims=True))
    soft = e / jnp.sum(e, axis=0, keepdims=True)
    lb = soft[0:1, :]
    for l in range(1, layer + 1):
        lb = lb + soft[l:l + 1, :]
    lb = lb - soft[0:1, :]

    ff = f_ref[...]
    t = jnp.exp(-jnp.abs(ff))
    a = jnp.log(lb)
    b = jnp.log(1.0 - lb) + (jnp.minimum(ff, 0.0) - jnp.log(1.0 + t))
    log_f = jnp.maximum(a, b) + _log1p_exp(a - b)
    kk = (1.0 - lb) * (jnp.where(ff >= 0.0, t, 1.0) / (1.0 + t))
    qq = _silu(q_ref[...])
    vv = i_ref[...]
    gate = g_ref[...]
    onw = onw_ref[...]

    chunks = [slice(r, r + CHUNK) for r in range(0, q_ref.shape[0], CHUNK)]
    hi, lo = _split(log_f)
    runs = [jnp.dot(sel_ref[...], jnp.concatenate([hi[rs], lo[rs]], axis=0), preferred_element_type=F32)
            for rs in chunks]

    def block(b):
        return jnp.concatenate([r[b * CHUNK:(b + 1) * CHUNK] for r in runs], axis=0)

    q_dec = (qq * jnp.exp(block(0))).astype(BF16)
    k_dec = (kk * jnp.exp(block(1))).astype(BF16)
    g_lasts = [jnp.exp(r[CHUNK - 1:CHUNK, :]) for r in runs]

    rrow = lax.broadcasted_iota(jnp.int32, qq.shape, 0)
    heads = range(q_ref.shape[1] // HEAD)
    lanes = [slice(h * HEAD, (h + 1) * HEAD) for h in heads]
    lanes2 = [slice(h * HEAD, (h + 2) * HEAD) for h in heads[::2]]
    duos = [(rs, l2) for rs in chunks for l2 in lanes2]
    row2 = lax.broadcasted_iota(jnp.int32, (CHUNK, 2 * CHUNK), 0)
    lane2 = lax.broadcasted_iota(jnp.int32, (CHUNK, 2 * CHUNK), 1)
    left = lane2 < CHUNK
    col2 = lane2 & (CHUNK - 1)
    first = lax.broadcasted_iota(jnp.int32, (CHUNK, 2 * HEAD), 1) < HEAD

    def key_blocks(t):
        return jnp.concatenate([jnp.where(first, t, 0), jnp.where(first, 0, t)], axis=0)

    def score_blocks(t):
        return jnp.concatenate([jnp.where(left, t, 0), jnp.where(left, 0, t)], axis=0)

    qk = qq * kk
    scores = [jnp.where(row2 == col2,
                        jnp.where(left, jnp.sum(qk[rs, l2][:, :HEAD], axis=1, keepdims=True),
                                  jnp.sum(qk[rs, l2][:, HEAD:], axis=1, keepdims=True)), 0.0)
              for rs, l2 in duos]
    s, level = CHUNK, 2
    while s > 1:
        upper = (rrow & (s - 1)) >= s // 2
        same = (row2 & -s) == (col2 & -s)
        factor = jnp.exp(block(level))
        qt = jnp.where(upper, qq * factor, 0.0).astype(BF16)
        kt = jnp.where(upper, 0.0, kk * factor).astype(BF16)
        scores = [a + jnp.where(same, _mm_nt(qt[rs, l2], key_blocks(kt[rs, l2])), 0.0)
                  for a, (rs, l2) in zip(scores, duos)]
        s, level = s // 2, level + 1
    scores = [score_blocks(a.astype(BF16)) for a in scores]

    states = [st_ref[h] for h in heads]
    for i, rs in enumerate(chunks):
        intra = [_mm(scores[i * len(lanes2) + d], jnp.concatenate([vv[rs, lanes[2 * d]], vv[rs, lanes[2 * d + 1]]], axis=0))
                 for d in range(len(lanes2))]
        outs = [_mm_nt(q_dec[rs, ln], st) + intra[h // 2][(h % 2) * CHUNK:(h % 2 + 1) * CHUNK]
                for h, (ln, st) in enumerate(zip(lanes, states))]
        states = [st * g_lasts[i][:, ln] + _mm_tn(vv[rs, ln], k_dec[rs, ln]) for ln, st in zip(lanes, states)]
        for h, ln in enumerate(lanes):
            o_ref[rs, ln] = _rms(outs[h], onw) * _silu(gate[rs, ln])
    for h in heads:
        st_ref[h] = states[h]


def _run_selectors():
    i = lax.broadcasted_iota(jnp.int32, (CHUNK, CHUNK), 0)
    r = lax.broadcasted_iota(jnp.int32, (CHUNK, CHUNK), 1)
    blocks = [r <= i, r > i]
    s = CHUNK
    while s > 1:
        mid = (i & -s) + s // 2
        blocks.append(jnp.where(i >= mid, (r > mid) & (r <= i), (r > i) & (r <= mid)))
        s //= 2
    sel = jnp.concatenate(blocks, axis=0).astype(BF16)
    return jnp.concatenate([sel, sel], axis=1)


def _hgrn(proj, lower_bounds, layer, out_norm, batch, seq):
    n = proj.shape[0]
    w = proj.shape[1] // 4
    gw = HGRN_GROUP * HEAD
    ng = w // gw
    tr = HGRN_CHUNKS * CHUNK
    nc = seq // tr
    sel = _run_selectors()
    rows = lambda b, c: b * nc + c
    return pl.pallas_call(
        functools.partial(_hgrn_kernel, layer=layer),
        grid=(batch, ng, nc),
        in_specs=[
            pl.BlockSpec((tr, gw), lambda b, g, c: (rows(b, c), g)),
            pl.BlockSpec((tr, gw), lambda b, g, c: (rows(b, c), ng + g)),
            pl.BlockSpec((tr, gw), lambda b, g, c: (rows(b, c), 2 * ng + g)),
            pl.BlockSpec((tr, gw), lambda b, g, c: (rows(b, c), 3 * ng + g)),
            pl.BlockSpec((lower_bounds.shape[0], gw), lambda b, g, c: (0, g)),
            pl.BlockSpec((1, HEAD), lambda b, g, c: (0, 0)),
            pl.BlockSpec(sel.shape, lambda b, g, c: (0, 0)),
        ],
        out_specs=pl.BlockSpec((tr, gw), lambda b, g, c: (rows(b, c), g)),
        out_shape=jax.ShapeDtypeStruct((n, w), F32),
        scratch_shapes=[pltpu.VMEM((HGRN_GROUP, HEAD, HEAD), F32)],
        compiler_params=pltpu.CompilerParams(
            dimension_semantics=("parallel", "parallel", "arbitrary"), vmem_limit_bytes=VMEM_LIMIT),
        name="hgrn2",
    )(proj, proj, proj, proj, lower_bounds.astype(F32), out_norm.reshape(1, HEAD), sel)


def _even_mixer(h, norm_w, w_in, conv_w, a_log, dt_bias, out_norm, w_out, batch, seq):
    gw, sw, nh = GDN_HEADS * HEAD, SB_HEADS * HEAD, GDN_HEADS
    o1 = 4 * gw
    o2 = o1 + 2 * nh
    w_gdn = jnp.concatenate([w_in[:, :o1], w_in[:, o1:o2],
                             jnp.zeros((w_in.shape[0], HEAD - 2 * nh), w_in.dtype)], axis=1)
    p_gdn = _norm_proj(h, norm_w, w_gdn, F32, tn=11 * HEAD)
    scale = jnp.concatenate([jnp.full((sw,), HEAD ** -0.5, F32), jnp.ones((2 * sw,), F32)])
    p_sb = _norm_proj(h, norm_w, w_in[:, o2:], BF16, tn=8 * HEAD, col_scale=scale)
    o_a = _gdn(p_gdn, conv_w, a_log, dt_bias, out_norm, batch, seq)
    o_b = _stick_breaking(p_sb, batch, seq)
    return _out_proj(h, [o_a, o_b], [w_out[:gw], w_out[gw:]])


def _odd_mixer(h, norm_w, w_in, lower_bounds, layer, out_norm, w_out, batch, seq):
    proj = _norm_proj(h, norm_w, w_in, F32, tn=16 * HEAD)
    o = _hgrn(proj, lower_bounds, layer, out_norm, batch, seq)
    return _out_proj(h, [o], [w_out])


def kernel(x, ffn1_norm, ffn1_w_gate, ffn1_w_up, ffn1_w_down, mix_norm, ffn2_norm, ffn2_w_gate, ffn2_w_up, ffn2_w_down, even_w_in, gdn_conv_w, gdn_a_log, gdn_dt_bias, gdn_out_norm, even_w_out, odd_w_in, hgrn_lower_bounds, hgrn_out_norm, odd_w_out, final_norm):
    batch, seq, d = x.shape
    depth = ffn1_norm.shape[0]
    h = x.reshape(batch * seq, d)
    for layer in range(depth):
        h = _ffn(h, ffn1_norm[layer], ffn1_w_gate[layer], ffn1_w_up[layer], ffn1_w_down[layer])
        m = layer // 2
        if layer % 2 == 0:
            h = _even_mixer(h, mix_norm[layer], even_w_in[m], gdn_conv_w[m], gdn_a_log[m], gdn_dt_bias[m],
                            gdn_out_norm[m], even_w_out[m], batch, seq)
        else:
            h = _odd_mixer(h, mix_norm[layer], odd_w_in[m], hgrn_lower_bounds, layer, hgrn_out_norm[m],
                           odd_w_out[m], batch, seq)
        h = _ffn(h, ffn2_norm[layer], ffn2_w_gate[layer], ffn2_w_up[layer], ffn2_w_down[layer],
                 final_w=final_norm if layer == depth - 1 else None)
    return h.reshape(batch, seq, d)
```
